```python
import jax
import jax.numpy as jnp
from jax import lax
import numpy as np


D_MODEL = 1024
BATCH = 1
SEQ = 16384
DEPTH = 4

GRID_W = 64
CTX_LEN = 256
N_MIXERS = 3
MIX_CONV = 0
MIX_ATTN = 1
MIX_MLSTM = 2
NORM_EPS = 1e-6
FFN_HIDDEN = 2816
CONV_WIDTH = 3
ATTN_HEAD_DIM = 128
ATTN_Q_HEADS = D_MODEL // ATTN_HEAD_DIM
ATTN_KV_HEADS = 2
ATTN_GROUP = ATTN_Q_HEADS // ATTN_KV_HEADS
ATTN_Q_DIM = ATTN_Q_HEADS * ATTN_HEAD_DIM
ATTN_KV_DIM = ATTN_KV_HEADS * ATTN_HEAD_DIM
ATTN_QKV_DIM = ATTN_Q_DIM + 2 * ATTN_KV_DIM
Q_BLOCK = 128
ROPE_THETA = 10000.0
MLSTM_HEADS = 4
MLSTM_DV = D_MODEL // MLSTM_HEADS
MLSTM_DK = MLSTM_DV // 2
MLSTM_CHUNK = 64
MLSTM_KD = MLSTM_HEADS * MLSTM_DK
MLSTM_N_GATES = 4 * MLSTM_HEADS
MLSTM_KVG_DIM = MLSTM_KD + D_MODEL + MLSTM_N_GATES
MLSTM_IN_DIM = MLSTM_KVG_DIM + MLSTM_KD + D_MODEL

kernel_name = 'hybrid_conv_gqa_mlstm_dit'


def _ctx_read_at_or_after(i):
    return any((j % N_MIXERS) != MIX_CONV for j in range(i, DEPTH))


def _rms(t, g):
    t32 = t.astype(jnp.float32)
    y = t32 * lax.rsqrt(jnp.mean(t32 * t32, axis=-1, keepdims=True) + NORM_EPS)
    return y.astype(t.dtype) * g


def _modulate(t, g, shift, scale):
    return _rms(t, g) * (1 + scale) + shift


def _swiglu(h, w13, w2):
    gte, up = jnp.split(h @ w13, 2, axis=-1)
    return (jax.nn.silu(gte) * up) @ w2


def _dwconv3(t, k):
    rhs = k[:, None, :].astype(t.dtype)
    pad = CONV_WIDTH // 2
    return lax.conv_general_dilated(t, rhs, window_strides=(1,), padding=[(pad, pad)],
                                    dimension_numbers=('NWC', 'WIO', 'NWC'),
                                    feature_group_count=t.shape[-1])


def _short_conv_mixer(h, w_in, k, w_out):
    bgate, cgate, xv = jnp.split(h @ w_in, 3, axis=-1)
    return (bgate * _dwconv3(cgate * xv, k)) @ w_out


def _axial_rope_tables(n_tok, dtype):
    rows = n_tok // GRID_W
    row = jnp.repeat(jnp.arange(rows), GRID_W).astype(jnp.float32)
    col = jnp.tile(jnp.arange(GRID_W), rows).astype(jnp.float32)
    seg = ATTN_HEAD_DIM // 2
    inv = ROPE_THETA ** (-jnp.arange(seg // 2, dtype=jnp.float32) / (seg // 2))
    ang_r = row[:, None] * inv
    ang_c = col[:, None] * inv
    ang = jnp.concatenate([ang_r, ang_r, ang_c, ang_c], axis=-1)
    return jnp.cos(ang).astype(dtype), jnp.sin(ang).astype(dtype)


def _apply_rope(t, cos, sin):
    ts = t.reshape(t.shape[:-1] + (2, 2, ATTN_HEAD_DIM // 4))
    rot = jnp.stack([-ts[..., 1, :], ts[..., 0, :]], axis=-2).reshape(t.shape)
    return t * cos[:, None, :] + rot * sin[:, None, :]


def _attend(q, k, v):
    s = jnp.einsum('bqhgd,bkhd->bhgqk', q, k).astype(jnp.float32) * (ATTN_HEAD_DIM ** -0.5)
    p = jax.nn.softmax(s, axis=-1).astype(v.dtype)
    return jnp.einsum('bhgqk,bkhd->bqhgd', p, v)


def _gqa_mixer(h, hc, with_ctx_out, w_qkv, q_g, k_g, w_o, cos, sin):
    bsz, t = h.shape[:2]
    tc = hc.shape[1]
    w_q, w_kv = w_qkv[:, :ATTN_Q_DIM], w_qkv[:, ATTN_Q_DIM:]

    def proj_q(u):
        return _rms((u @ w_q).reshape(u.shape[0], u.shape[1], ATTN_Q_HEADS, ATTN_HEAD_DIM), q_g)

    def proj_kv(u):
        kv = u @ w_kv
        kk = _rms(kv[..., :ATTN_KV_DIM].reshape(u.shape[0], u.shape[1], ATTN_KV_HEADS, ATTN_HEAD_DIM), k_g)
        vv = kv[..., ATTN_KV_DIM:].reshape(u.shape[0], u.shape[1], ATTN_KV_HEADS, ATTN_HEAD_DIM)
        return kk, vv

    q = _apply_rope(proj_q(h), cos, sin)
    k, v = proj_kv(h)
    k = _apply_rope(k, cos, sin)
    kc, vc = proj_kv(hc)
    k_all = jnp.concatenate([kc, k], axis=1)
    v_all = jnp.concatenate([vc, v], axis=1)
    nblk = t // Q_BLOCK
    qb = q.reshape(bsz, nblk, Q_BLOCK, ATTN_KV_HEADS, ATTN_GROUP, ATTN_HEAD_DIM).swapaxes(0, 1)
    ob = lax.map(lambda qi: _attend(qi, k_all, v_all), qb)
    y = ob.swapaxes(0, 1).reshape(bsz, t, D_MODEL) @ w_o
    yc = None
    if with_ctx_out:
        qc = proj_q(hc).reshape(bsz, tc, ATTN_KV_HEADS, ATTN_GROUP, ATTN_HEAD_DIM)
        yc = _attend(qc, kc, vc).reshape(bsz, tc, D_MODEL) @ w_o
    return y, yc


def _mlstm_scan(q, k, v, ig, fg, state):
    with_out = q is not None
    bsz, t, nh = ig.shape
    L = MLSTM_CHUNK
    nc = t // L

    def chunks(a):
        return a.reshape((bsz, nc, L) + a.shape[2:]).swapaxes(0, 1)

    causal = jnp.tril(jnp.ones((L, L), dtype=bool))

    def step(carry, xs):
        C, n, m = carry
        if with_out:
            qc, kc, vc, ic, fc = xs
        else:
            kc, vc, ic, fc = xs
        A = jnp.cumsum(jax.nn.log_sigmoid(fc), axis=1)
        A_end = A[:, -1]
        w_end = A_end[:, None] - A + ic
        m_new = jnp.maximum(A_end + m, jnp.max(w_end, axis=1))
        e_end = jnp.exp(w_end - m_new[:, None])
        decay = jnp.exp(A_end + m - m_new)
        C_new = decay[..., None, None] * C + jnp.einsum('bsh,bshd,bshv->bhdv', e_end, kc, vc)
        n_new = decay[..., None] * n + jnp.einsum('bsh,bshd->bhd', e_end, kc)
        if not with_out:
            return (C_new, n_new, m_new), None
        Dm = A[:, :, None] - A[:, None] + ic[:, None]
        Dm = jnp.where(causal[None, :, :, None], Dm, -jnp.inf)
        inter = A + m[:, None]
        m_t = jnp.maximum(jnp.max(Dm, axis=2), inter)
        w = jnp.exp(Dm - m_t[:, :, None])
        s = jnp.einsum('bthd,bshd->btsh', qc, kc) * w
        sc = jnp.exp(inter - m_t)
        h_num = jnp.einsum('btsh,bshv->bthv', s, vc) + sc[..., None] * jnp.einsum('bthd,bhdv->bthv', qc, C)
        nq = jnp.sum(s, axis=2) + sc * jnp.einsum('bthd,bhd->bth', qc, n)
        hh = h_num / jnp.maximum(jnp.abs(nq), jnp.exp(-m_t))[..., None]
        return (C_new, n_new, m_new), hh

    xs = (chunks(k), chunks(v), chunks(ig), chunks(fg))
    if with_out:
        xs = (chunks(q),) + xs
    state, hs = lax.scan(step, state, xs)
    if with_out:
        hs = hs.swapaxes(0, 1).reshape(bsz, t, nh, MLSTM_DV)
    return state, hs


def _mlstm_split_kvg(p, b_gate):
    bsz, t = p.shape[:2]
    k = p[..., :MLSTM_KD].reshape(bsz, t, MLSTM_HEADS, MLSTM_DK).astype(jnp.float32) * (MLSTM_DK ** -0.5)
    v = p[..., MLSTM_KD:MLSTM_KD + D_MODEL].reshape(bsz, t, MLSTM_HEADS, MLSTM_DV).astype(jnp.float32)
    g = (p[..., MLSTM_KD + D_MODEL:MLSTM_KVG_DIM] + b_gate.reshape(-1)).astype(jnp.float32)
    return k, v, g.reshape(bsz, t, 4, MLSTM_HEADS)


def _mlstm_split_qo(p):
    bsz, t = p.shape[:2]
    q = p[..., MLSTM_KVG_DIM:MLSTM_KVG_DIM + MLSTM_KD].reshape(bsz, t, MLSTM_HEADS, MLSTM_DK).astype(jnp.float32)
    return q, p[..., MLSTM_KVG_DIM + MLSTM_KD:]


def _rev(a, d):
    return jnp.flip(a, axis=1) if d == 1 else a


def _mlstm_out(hh, o, norm_g, w_o):
    bsz, t = hh.shape[:2]
    hn = _rms(hh.astype(o.dtype), norm_g.reshape(MLSTM_HEADS, MLSTM_DV)).reshape(bsz, t, D_MODEL)
    return (jax.nn.sigmoid(o) * hn) @ w_o


def _mlstm_mixer(h, hc, with_ctx_out, w_in, b_gate, norm_g, w_o):
    bsz = h.shape[0]
    p = h @ w_in
    k, v, g = _mlstm_split_kvg(p, b_gate)
    q, o = _mlstm_split_qo(p)
    if with_ctx_out:
        pc = hc @ w_in
        qc, oc = _mlstm_split_qo(pc)
    else:
        pc = hc @ w_in[:, :MLSTM_KVG_DIM]
        qc, oc = None, None
    kc, vc, gc = _mlstm_split_kvg(pc, b_gate)
    zero = (jnp.zeros((bsz, MLSTM_HEADS, MLSTM_DK, MLSTM_DV), jnp.float32),
            jnp.zeros((bsz, MLSTM_HEADS, MLSTM_DK), jnp.float32),
            jnp.zeros((bsz, MLSTM_HEADS), jnp.float32))
    lat_out, ctx_out = [], []
    for d in range(2):
        ig, fg = 2 * d, 2 * d + 1
        st_ctx, y_ctx = _mlstm_scan(_rev(qc, d) if with_ctx_out else None, _rev(kc, d), _rev(vc, d),
                                    _rev(gc[:, :, ig], d), _rev(gc[:, :, fg], d), zero)
        _, y_lat = _mlstm_scan(_rev(q, d), _rev(k, d), _rev(v, d),
                               _rev(g[:, :, ig], d), _rev(g[:, :, fg], d), st_ctx)
        lat_out.append(_rev(y_lat, d))
        if with_ctx_out:
            ctx_out.append(_rev(y_ctx, d))
    y = _mlstm_out(lat_out[0] + lat_out[1], o, norm_g, w_o)
    yc = _mlstm_out(ctx_out[0] + ctx_out[1], oc, norm_g, w_o) if with_ctx_out else None
    return y, yc


def setup_inputs(seed: int = 0) -> dict:
    key = jax.random.key(seed)
    keys = list(jax.random.split(key, 32))
    f32 = jnp.float32

    def nk():
        return keys.pop()

    def w(shape, fan_in, scale=1.0):
        return jax.random.normal(nk(), shape, f32) * (scale * fan_in ** -0.5)

    n_conv = sum(1 for i in range(DEPTH) if i % N_MIXERS == MIX_CONV)
    n_attn = sum(1 for i in range(DEPTH) if i % N_MIXERS == MIX_ATTN)
    n_mlstm = sum(1 for i in range(DEPTH) if i % N_MIXERS == MIX_MLSTM)
    D = D_MODEL
    gate_bias = jnp.concatenate([
        -3.0 + 0.1 * jax.random.normal(nk(), (n_mlstm, 1, MLSTM_HEADS), f32),
        3.0 + 3.0 * jax.random.uniform(nk(), (n_mlstm, 1, MLSTM_HEADS), f32),
        -3.0 + 0.1 * jax.random.normal(nk(), (n_mlstm, 1, MLSTM_HEADS), f32),
        3.0 + 3.0 * jax.random.uniform(nk(), (n_mlstm, 1, MLSTM_HEADS), f32)], axis=1)
    return {
        'x': jax.random.normal(nk(), (BATCH, SEQ, D), f32),
        'c': jax.random.normal(nk(), (BATCH, D), f32),
        'ctx': jax.random.normal(nk(), (BATCH, CTX_LEN, D), f32),
        'c_ctx': jax.random.normal(nk(), (D,), f32),
        'mod_w': w((DEPTH, D, 9 * D), D, 0.5),
        'mod_b': 0.02 * jax.random.normal(nk(), (DEPTH, 9 * D), f32),
        'norm_g': 1.0 + 0.05 * jax.random.normal(nk(), (DEPTH, 3, D), f32),
        'ffn_w13': w((DEPTH, 2, D, 2 * FFN_HIDDEN), D),
        'ffn_w2': w((DEPTH, 2, FFN_HIDDEN, D), FFN_HIDDEN),
        'conv_w_in': w((n_conv, D, 3 * D), D),
        'conv_k': w((n_conv, CONV_WIDTH, D), CONV_WIDTH),
        'conv_w_out': w((n_conv, D, D), D),
        'attn_w_qkv': w((n_attn, D, ATTN_QKV_DIM), D),
        'attn_q_g': 1.0 + 0.05 * jax.random.normal(nk(), (n_attn, ATTN_HEAD_DIM), f32),
        'attn_k_g': 1.0 + 0.05 * jax.random.normal(nk(), (n_attn, ATTN_HEAD_DIM), f32),
        'attn_w_o': w((n_attn, D, D), D),
        'mlstm_w_in': w((n_mlstm, D, MLSTM_IN_DIM), D),
        'mlstm_b_gate': gate_bias,
        'mlstm_norm_g': 1.0 + 0.05 * jax.random.normal(nk(), (n_mlstm, D), f32),
        'mlstm_w_o': w((n_mlstm, D, D), D),
    }


def reference(x, c, ctx, c_ctx, mod_w, mod_b, norm_g, ffn_w13, ffn_w2,
              conv_w_in, conv_k, conv_w_out,
              attn_w_qkv, attn_q_g, attn_k_g, attn_w_o,
              mlstm_w_in, mlstm_b_gate, mlstm_norm_g, mlstm_w_o):
    bsz, n_tok, _ = x.shape
    cos, sin = _axial_rope_tables(n_tok, x.dtype)
    lat, cx = x, ctx
    counters = [0, 0, 0]
    for i in range(DEPTH):
        kind = i % N_MIXERS
        j = counters[kind]
        counters[kind] += 1
        ctx_out = _ctx_read_at_or_after(i + 1)
        if not _ctx_read_at_or_after(i):
            cx = None
        mod = (jax.nn.silu(c) @ mod_w[i] + mod_b[i]).reshape(bsz, 3, 3, 1, D_MODEL)
        modc = (jax.nn.silu(c_ctx) @ mod_w[i] + mod_b[i]).reshape(3, 3, D_MODEL)
        lat = lat + 0.5 * mod[:, 0, 2] * _swiglu(_modulate(lat, norm_g[i, 0], mod[:, 0, 0], mod[:, 0, 1]), ffn_w13[i, 0], ffn_w2[i, 0])
        if cx is not None:
            cx = cx + 0.5 * modc[0, 2] * _swiglu(_modulate(cx, norm_g[i, 0], modc[0, 0], modc[0, 1]), ffn_w13[i, 0], ffn_w2[i, 0])
        h = _modulate(lat, norm_g[i, 1], mod[:, 1, 0], mod[:, 1, 1])
        hc = _modulate(cx, norm_g[i, 1], modc[1, 0], modc[1, 1]) if cx is not None else None
        if kind == MIX_CONV:
            y = _short_conv_mixer(h, conv_w_in[j], conv_k[j], conv_w_out[j])
            yc = _short_conv_mixer(hc, conv_w_in[j], conv_k[j], conv_w_out[j]) if ctx_out else None
        elif kind == MIX_ATTN:
            y, yc = _gqa_mixer(h, hc, ctx_out, attn_w_qkv[j], attn_q_g[j], attn_k_g[j], attn_w_o[j], cos, sin)
        else:
            y, yc = _mlstm_mixer(h, hc, ctx_out, mlstm_w_in[j], mlstm_b_gate[j], mlstm_norm_g[j], mlstm_w_o[j])
        lat = lat + mod[:, 1, 2] * y
        if ctx_out:
            cx = cx + modc[1, 2] * yc
            cx = cx + 0.5 * modc[2, 2] * _swiglu(_modulate(cx, norm_g[i, 2], modc[2, 0], modc[2, 1]), ffn_w13[i, 1], ffn_w2[i, 1])
        else:
            cx = None
        lat = lat + 0.5 * mod[:, 2, 2] * _swiglu(_modulate(lat, norm_g[i, 2], mod[:, 2, 0], mod[:, 2, 1]), ffn_w13[i, 1], ffn_w2[i, 1])
    return lat
```

```python
import functools

import jax
import jax.numpy as jnp
from jax import lax
from jax.experimental import pallas as pl
from jax.experimental.pallas import tpu as pltpu

F32 = jnp.float32
BF16 = jnp.bfloat16

D_MODEL = 1024
DEPTH = 4
N_MIXERS = 3
NORM_EPS = 1e-6
FFN_HIDDEN = 2816
FFN_CHUNK = 256
GRID_W = 64
HEAD_DIM = 128
Q_HEADS = 8
KV_HEADS = 2
GROUP = Q_HEADS // KV_HEADS
ROPE_THETA = 10000.0
ML_HEADS = 4
ML_DV = 256
ML_DK = 128
ML_KD = ML_HEADS * ML_DK
ML_GATES = 4 * ML_HEADS
ML_CHUNK = 128
ML_STATE_W = ML_DV + 128
NEG_BIG = -1e30

VMEM_LIMIT = 56 * 1024 * 1024


def _cparams(*sem):
    return pltpu.CompilerParams(dimension_semantics=sem, vmem_limit_bytes=VMEM_LIMIT)


def _full(shape):
    return pl.BlockSpec(shape, lambda *_: (0,) * len(shape))


def _modulated(x, vec_ref):
    ms = jnp.mean(x * x, axis=-1, keepdims=True)
    y = x * lax.rsqrt(ms + NORM_EPS) * vec_ref[3:4, :]
    return y * (1.0 + vec_ref[1:2, :]) + vec_ref[0:1, :]


def _sigmoid(x):
    return 1.0 / (1.0 + jnp.exp(-x))


def _log_sigmoid(x):
    return jnp.minimum(x, 0.0) - jnp.log(1.0 + jnp.exp(-jnp.abs(x)))


def _mod_kernel(c_ref, w_ref, b_ref, o_ref):
    c = c_ref[...]
    s = (c * _sigmoid(c)).astype(BF16)
    o_ref[0] = jnp.dot(s, w_ref[0].astype(BF16), preferred_element_type=F32) + b_ref[0]


def _mod_call(cvec, mod_w, mod_b):
    depth, d, n = mod_w.shape
    tn = 1024
    return pl.pallas_call(
        _mod_kernel,
        grid=(depth, n // tn),
        in_specs=[_full((8, d)),
                  pl.BlockSpec((1, d, tn), lambda l, j: (l, 0, j)),
                  pl.BlockSpec((1, 1, tn), lambda l, j: (l, 0, j))],
        out_specs=pl.BlockSpec((1, 8, tn), lambda l, j: (l, 0, j)),
        out_shape=jax.ShapeDtypeStruct((depth, 8, n), F32),
        compiler_params=_cparams("arbitrary", "arbitrary"),
        name="adaln_mod",
    )(cvec, mod_w, mod_b.reshape(depth, 1, n))


def _ffn_kernel(x_ref, vec_ref, w13_ref, w2_ref, o_ref, acc_ref):
    x = x_ref[...]
    h = _modulated(x, vec_ref).astype(BF16)
    for j in range(FFN_HIDDEN // FFN_CHUNK):
        lo = j * FFN_CHUNK
        a = jnp.dot(h, w13_ref[:, lo:lo + FFN_CHUNK], preferred_element_type=F32)
        b = jnp.dot(h, w13_ref[:, FFN_HIDDEN + lo:FFN_HIDDEN + lo + FFN_CHUNK], preferred_element_type=F32)
        u = (a * _sigmoid(a) * b).astype(BF16)
        y = jnp.dot(u, w2_ref[lo:lo + FFN_CHUNK, :], preferred_element_type=F32)
        if j == 0:
            acc_ref[...] = y
        else:
            acc_ref[...] += y
    o_ref[...] = x + (0.5 * vec_ref[2:3, :]) * acc_ref[...]


def _ffn_call(x, vec, w13, w2, tm):
    t, d = x.shape
    return pl.pallas_call(
        _ffn_kernel,
        grid=(t // tm,),
        in_specs=[pl.BlockSpec((tm, d), lambda i: (i, 0)),
                  _full((8, d)),
                  _full(w13.shape),
                  _full(w2.shape)],
        out_specs=pl.BlockSpec((tm, d), lambda i: (i, 0)),
        out_shape=jax.ShapeDtypeStruct((t, d), F32),
        scratch_shapes=[pltpu.VMEM((tm, d), F32)],
        compiler_params=_cparams("arbitrary"),
        name="swiglu_half_step",
    )(x, vec, w13, w2)


def _proj_res_kernel(x_ref, a_ref, vec_ref, w_ref, o_ref):
    y = jnp.dot(a_ref[...], w_ref[...], preferred_element_type=F32)
    o_ref[...] = x_ref[...] + vec_ref[2:3, :] * y


def _proj_res_call(x, a, vec, w, tm):
    t, d = x.shape
    return pl.pallas_call(
        _proj_res_kernel,
        grid=(t // tm,),
        in_specs=[pl.BlockSpec((tm, d), lambda i: (i, 0)),
                  pl.BlockSpec((tm, a.shape[1]), lambda i: (i, 0)),
                  _full((8, d)),
                  _full(w.shape)],
        out_specs=pl.BlockSpec((tm, d), lambda i: (i, 0)),
        out_shape=jax.ShapeDtypeStruct((t, d), F32),
        compiler_params=_cparams("arbitrary"),
        name="out_proj_residual",
    )(x, a, vec, w)


CONV_HALO = 16


def _conv_kernel(x_ref, xp_ref, xn_ref, vec_ref, win_ref, k_ref, wout_ref, o_ref, h_ref, acc_ref, *, tm):
    i = pl.program_id(0)
    last = pl.num_programs(0) - 1
    x = x_ref[...]
    d = x.shape[1]
    keep_prev = (i > 0).astype(F32)
    keep_next = (i < last).astype(F32)
    h_ref[0:CONV_HALO, :] = (_modulated(xp_ref[...], vec_ref) * keep_prev).astype(BF16)
    h_ref[CONV_HALO:CONV_HALO + tm, :] = _modulated(x, vec_ref).astype(BF16)
    h_ref[CONV_HALO + tm:, :] = (_modulated(xn_ref[...], vec_ref) * keep_next).astype(BF16)
    rows = tm + 2 * CONV_HALO
    cw = 256
    for j in range(d // cw):
        lo = j * cw
        hall = h_ref[...]
        cg = jnp.dot(hall, win_ref[:, d + lo:d + lo + cw], preferred_element_type=F32)
        xv = jnp.dot(hall, win_ref[:, 2 * d + lo:2 * d + lo + cw], preferred_element_type=F32)
        u = cg * xv
        up = pltpu.roll(u, 1, axis=0)
        un = pltpu.roll(u, rows - 1, axis=0)
        kk = k_ref[:, lo:lo + cw]
        conv = kk[0:1, :] * up + kk[1:2, :] * u + kk[2:3, :] * un
        conv = conv[CONV_HALO:CONV_HALO + tm, :]
        bg = jnp.dot(h_ref[CONV_HALO:CONV_HALO + tm, :], win_ref[:, lo:lo + cw], preferred_element_type=F32)
        y = jnp.dot((bg * conv).astype(BF16), wout_ref[lo:lo + cw, :], preferred_element_type=F32)
        if j == 0:
            acc_ref[...] = y
        else:
            acc_ref[...] += y
    o_ref[...] = x + vec_ref[2:3, :] * acc_ref[...]


def _conv_call(x, vec, w_in, k8, w_out, tm):
    t, d = x.shape
    hb = tm // CONV_HALO
    nhb = t // CONV_HALO
    return pl.pallas_call(
        functools.partial(_conv_kernel, tm=tm),
        grid=(t // tm,),
        in_specs=[pl.BlockSpec((tm, d), lambda i: (i, 0)),
                  pl.BlockSpec((CONV_HALO, d), lambda i: (jnp.maximum(i * hb - 1, 0), 0)),
                  pl.BlockSpec((CONV_HALO, d), lambda i: (jnp.minimum((i + 1) * hb, nhb - 1), 0)),
                  _full((8, d)),
                  _full(w_in.shape),
                  _full((8, d)),
                  _full(w_out.shape)],
        out_specs=pl.BlockSpec((tm, d), lambda i: (i, 0)),
        out_shape=jax.ShapeDtypeStruct((t, d), F32),
        scratch_shapes=[pltpu.VMEM((tm + 2 * CONV_HALO, d), BF16), pltpu.VMEM((tm, d), F32)],
        compiler_params=_cparams("arbitrary"),
        name="short_conv_mixer",
    )(x, x, x, vec, w_in, k8, w_out)


def _head_rms(t, g):
    return t * lax.rsqrt(jnp.mean(t * t, axis=-1, keepdims=True) + NORM_EPS) * g


def _rope(t, cos, sin):
    lane = lax.broadcasted_iota(jnp.int32, t.shape, 1)
    first = (lane % (HEAD_DIM // 2)) < (HEAD_DIM // 4)
    rot = jnp.where(first, -pltpu.roll(t, HEAD_DIM - HEAD_DIM // 4, axis=1), pltpu.roll(t, HEAD_DIM // 4, axis=1))
    return t * cos + rot * sin


def _qkv_kernel(x_ref, vec_ref, w_ref, g_ref, cos_ref, sin_ref, q_ref, k_ref, v_ref):
    h = _modulated(x_ref[...], vec_ref).astype(BF16)
    p = jnp.dot(h, w_ref[...], preferred_element_type=F32)
    cos = cos_ref[...]
    sin = sin_ref[...]
    tm = h.shape[0]
    for hq in range(Q_HEADS):
        t = _head_rms(p[:, hq * HEAD_DIM:(hq + 1) * HEAD_DIM], g_ref[0:1, :])
        q_ref[hq // GROUP, hq % GROUP] = _rope(t, cos, sin).astype(BF16)
    qd = Q_HEADS * HEAD_DIM
    kvd = KV_HEADS * HEAD_DIM
    for hk in range(KV_HEADS):
        t = _head_rms(p[:, qd + hk * HEAD_DIM:qd + (hk + 1) * HEAD_DIM], g_ref[1:2, :])
        k_ref[hk] = _rope(t, cos, sin).astype(BF16)
        v = p[:, qd + kvd + hk * HEAD_DIM:qd + kvd + (hk + 1) * HEAD_DIM]
        v_ref[hk] = jnp.concatenate([v, jnp.ones((tm, HEAD_DIM), F32)], axis=1).astype(BF16)


def _qkv_call(x, vec, w_qkv, g8, cos, sin, tm):
    t, d = x.shape
    return pl.pallas_call(
        _qkv_kernel,
        grid=(t // tm,),
        in_specs=[pl.BlockSpec((tm, d), lambda i: (i, 0)),
                  _full((8, d)),
                  _full(w_qkv.shape),
                  _full((8, HEAD_DIM)),
                  pl.BlockSpec((tm, HEAD_DIM), lambda i: (i, 0)),
                  pl.BlockSpec((tm, HEAD_DIM), lambda i: (i, 0))],
        out_specs=[pl.BlockSpec((KV_HEADS, GROUP, tm, HEAD_DIM), lambda i: (0, 0, i, 0)),
                   pl.BlockSpec((KV_HEADS, tm, HEAD_DIM), lambda i: (0, i, 0)),
                   pl.BlockSpec((KV_HEADS, tm, 2 * HEAD_DIM), lambda i: (0, i, 0))],
        out_shape=[jax.ShapeDtypeStruct((KV_HEADS, GROUP, t, HEAD_DIM), BF16),
                   jax.ShapeDtypeStruct((KV_HEADS, t, HEAD_DIM), BF16),
                   jax.ShapeDtypeStruct((KV_HEADS, t, 2 * HEAD_DIM), BF16)],
        compiler_params=_cparams("arbitrary"),
        name="gqa_qkv_proj",
    )(x, vec, w_qkv, g8, cos, sin)


def _flash_kernel(*refs, tq, tiles):
    q_ref = refs[0]
    kv_refs = refs[1:1 + 2 * len(tiles)]
    o_ref = refs[1 + 2 * len(tiles)]
    m_ref, acc_ref = refs[2 + 2 * len(tiles):]
    rows = GROUP * tq
    q = q_ref[0].reshape(rows, HEAD_DIM)
    c = (HEAD_DIM ** -0.5) * 1.4426950408889634
    m_ref[...] = jnp.full((rows, 1), NEG_BIG, F32)
    acc_ref[...] = jnp.zeros((rows, 2 * HEAD_DIM), F32)

    def attend(ks, vs):
        s = lax.dot_general(q, ks, (((1,), (1,)), ((), ())), preferred_element_type=F32)
        m_old = m_ref[...]
        m_new = jnp.maximum(m_old, jnp.max(s, axis=1, keepdims=True))
        alpha = jnp.exp2((m_old - m_new) * c)
        p = jnp.exp2(s * c - m_new * c).astype(BF16)
        acc_ref[...] = alpha * acc_ref[...] + jnp.dot(p, vs, preferred_element_type=F32)
        m_ref[...] = m_new

    for n, (tk, nk) in enumerate(tiles):
        k_ref, v_ref = kv_refs[2 * n], kv_refs[2 * n + 1]
        if nk == 1:
            attend(k_ref[0], v_ref[0])
        else:
            def body(j, carry, k_ref=k_ref, v_ref=v_ref, tk=tk):
                off = pl.multiple_of(j * tk, tk)
                attend(k_ref[0, pl.ds(off, tk), :], v_ref[0, pl.ds(off, tk), :])
                return carry
            lax.fori_loop(0, nk, body, 0)

    acc = acc_ref[...]
    out = acc[:, :HEAD_DIM] / acc[:, HEAD_DIM:]
    for g in range(GROUP):
        o_ref[:, g * HEAD_DIM:(g + 1) * HEAD_DIM] = out[g * tq:(g + 1) * tq].astype(o_ref.dtype)


def _flash_call(q, kvs, tq, tks):
    _, _, t, _ = q.shape
    tiles = tuple((tk, k.shape[1] // tk) for (k, _), tk in zip(kvs, tks))
    in_specs = [pl.BlockSpec((1, GROUP, tq, HEAD_DIM), lambda h, i: (h, 0, i, 0))]
    args = [q]
    for k, v in kvs:
        in_specs.append(pl.BlockSpec((1,) + k.shape[1:], lambda h, i: (h, 0, 0)))
        in_specs.append(pl.BlockSpec((1,) + v.shape[1:], lambda h, i: (h, 0, 0)))
        args += [k, v]
    return pl.pallas_call(
        functools.partial(_flash_kernel, tq=tq, tiles=tiles),
        grid=(KV_HEADS, t // tq),
        in_specs=in_specs,
        out_specs=pl.BlockSpec((tq, GROUP * HEAD_DIM), lambda h, i: (i, h)),
        out_shape=jax.ShapeDtypeStruct((t, Q_HEADS * HEAD_DIM), BF16),
        scratch_shapes=[pltpu.VMEM((GROUP * tq, 1), F32), pltpu.VMEM((GROUP * tq, 2 * HEAD_DIM), F32)],
        compiler_params=_cparams("arbitrary", "arbitrary"),
        name="gqa_flash",
    )(*args)


def _ml_proj_kernel(x_ref, vec_ref, wq_ref, wkt_ref, wv_ref, wg_ref, wgt_ref, bg_ref, bgt_ref,
                    q_ref, kt_ref, v_ref, gc_ref, gr_ref):
    h = _modulated(x_ref[...], vec_ref).astype(BF16)
    nt = (((1,), (1,)), ((), ()))
    q_ref[...] = jnp.dot(h, wq_ref[...], preferred_element_type=F32).astype(BF16)
    kt = lax.dot_general(wkt_ref[...], h, nt, preferred_element_type=F32)
    kt_ref[...] = (kt * (ML_DK ** -0.5)).astype(BF16)
    v_ref[...] = jnp.dot(h, wv_ref[...], preferred_element_type=F32).astype(BF16)
    gc_ref[...] = jnp.dot(h, wg_ref[...], preferred_element_type=F32) + bg_ref[...]
    gr_ref[...] = lax.dot_general(wgt_ref[...], h, nt, preferred_element_type=F32) + bgt_ref[...]


def _ml_proj_call(x, vec, wq, wkt, wv, wg, wgt, bg, bgt, tm):
    t, d = x.shape
    row = lambda i: (i, 0)
    col = lambda i: (0, i)
    return pl.pallas_call(
        _ml_proj_kernel,
        grid=(t // tm,),
        in_specs=[pl.BlockSpec((tm, d), row), _full((8, d)), _full(wq.shape), _full(wkt.shape), _full(wv.shape),
                  _full(wg.shape), _full(wgt.shape), _full(bg.shape), _full(bgt.shape)],
        out_specs=[pl.BlockSpec((tm, ML_KD), row), pl.BlockSpec((ML_KD, tm), col), pl.BlockSpec((tm, d), row),
                   pl.BlockSpec((tm, ML_GATES), row), pl.BlockSpec((ML_GATES, tm), col)],
        out_shape=[jax.ShapeDtypeStruct((t, ML_KD), BF16), jax.ShapeDtypeStruct((ML_KD, t), BF16),
                   jax.ShapeDtypeStruct((t, d), BF16), jax.ShapeDtypeStruct((t, ML_GATES), F32),
                   jax.ShapeDtypeStruct((ML_GATES, t), F32)],
        compiler_params=_cparams("arbitrary"),
        name="mlstm_in_proj",
    )(x, vec, wq, wkt, wv, wg, wgt, bg, bgt)


def _ml_scan_kernel(qf_ref, ktf_ref, vf_ref, gcf_ref, grf_ref, qb_ref, ktb_ref, vb_ref, gcb_ref, grb_ref,
                    c0_ref, m0_ref, hf_ref, hb_ref, cout_ref, mout_ref, c_ref, m_ref):
    n = pl.program_id(0)
    L = ML_CHUNK

    @pl.when(n == 0)
    def _():
        c_ref[...] = c0_ref[...]
        m_ref[...] = m0_ref[...]

    r = lax.broadcasted_iota(jnp.int32, (L, L), 0)
    s_ = lax.broadcasted_iota(jnp.int32, (L, L), 1)
    lower = r >= s_
    upper = r <= s_
    ones_blk = jnp.ones((L, ML_STATE_W - ML_DV), BF16)
    hi = lax.Precision.HIGHEST
    per_dir = ((qf_ref, ktf_ref, vf_ref, gcf_ref, grf_ref, hf_ref, lower, upper, L - 1),
               (qb_ref, ktb_ref, vb_ref, gcb_ref, grb_ref, hb_ref, upper, lower, 0))
    for d, (q_ref, kt_ref, v_ref, gc_ref, gr_ref, h_ref, seen, seen_t, end) in enumerate(per_dir):
        gcol = gc_ref[...]
        grow = gr_ref[...]
        a_col = jnp.dot(seen.astype(F32), _log_sigmoid(gcol), precision=hi, preferred_element_type=F32)
        a_row = jnp.dot(_log_sigmoid(grow), seen_t.astype(F32), precision=hi, preferred_element_type=F32)
        for hd in range(ML_HEADS):
            ji = (2 * d) * ML_HEADS + hd
            jf = (2 * d + 1) * ML_HEADS + hd
            st = d * ML_HEADS + hd
            a_c = a_col[:, jf:jf + 1]
            a_r = a_row[jf:jf + 1, :]
            b_r = grow[ji:ji + 1, :] - a_r
            a_end = a_r[:, end:end + 1]
            m_prev = m_ref[st:st + 1, 0:1]
            dm = jnp.where(seen, a_c + b_r, -jnp.inf)
            inter = a_c + m_prev
            m_t = jnp.maximum(jnp.max(dm, axis=1, keepdims=True), inter)
            w = jnp.exp(dm - m_t)
            qh = q_ref[:, hd * ML_DK:(hd + 1) * ML_DK]
            kth = kt_ref[hd * ML_DK:(hd + 1) * ML_DK, :]
            sw = jnp.dot(qh, kth, preferred_element_type=F32) * w
            sc = jnp.exp(inter - m_t)
            vp = jnp.concatenate([v_ref[:, hd * ML_DV:(hd + 1) * ML_DV], ones_blk], axis=1)
            cst = c_ref[st]
            hx = (jnp.dot(sw.astype(BF16), vp, preferred_element_type=F32)
                  + sc * jnp.dot(qh, cst.astype(BF16), preferred_element_type=F32))
            den = jnp.maximum(jnp.abs(hx[:, ML_DV:]), jnp.exp(-m_t))
            inv = 1.0 / den
            h_ref[:, hd * ML_DV:hd * ML_DV + 128] = hx[:, 0:128] * inv
            h_ref[:, hd * ML_DV + 128:(hd + 1) * ML_DV] = hx[:, 128:ML_DV] * inv
            w_end = a_end + b_r
            m_new = jnp.maximum(a_end + m_prev, jnp.max(w_end, axis=1, keepdims=True))
            e_r = jnp.exp(w_end - m_new)
            decay = jnp.exp(a_end + m_prev - m_new)
            ke = (kth.astype(F32) * e_r).astype(BF16)
            c_ref[st] = decay * cst + jnp.dot(ke, vp, preferred_element_type=F32)
            m_ref[st:st + 1, :] = jnp.broadcast_to(m_new, (1, 128))

    @pl.when(n == pl.num_programs(0) - 1)
    def _():
        cout_ref[...] = c_ref[...]
        mout_ref[...] = m_ref[...]


def _ml_scan_call(q, kt, v, gc, gr, c0, m0):
    t, d = v.shape
    L = ML_CHUNK
    nc = t // L
    fr = lambda n: (n, 0)
    fc = lambda n: (0, n)
    br = lambda n: (nc - 1 - n, 0)
    bc = lambda n: (0, nc - 1 - n)

    def specs(rm, cm):
        return [pl.BlockSpec((L, ML_KD), rm), pl.BlockSpec((ML_KD, L), cm), pl.BlockSpec((L, d), rm),
                pl.BlockSpec((L, ML_GATES), rm), pl.BlockSpec((ML_GATES, L), cm)]

    nst = 2 * ML_HEADS
    return pl.pallas_call(
        _ml_scan_kernel,
        grid=(nc,),
        in_specs=specs(fr, fc) + specs(br, bc) + [_full(c0.shape), _full(m0.shape)],
        out_specs=[pl.BlockSpec((L, d), fr), pl.BlockSpec((L, d), br), _full(c0.shape), _full(m0.shape)],
        out_shape=[jax.ShapeDtypeStruct((t, d), F32), jax.ShapeDtypeStruct((t, d), F32),
                   jax.ShapeDtypeStruct(c0.shape, F32), jax.ShapeDtypeStruct(m0.shape, F32)],
        scratch_shapes=[pltpu.VMEM((nst, ML_DK, ML_STATE_W), F32), pltpu.VMEM((nst, 128), F32)],
        compiler_params=_cparams("arbitrary"),
        name="mlstm_scan",
    )(q, kt, v, gc, gr, q, kt, v, gc, gr, c0, m0)


def _ml_out_kernel(x_ref, hf_ref, hb_ref, vec_ref, ng_ref, woi_ref, wo_ref, o_ref):
    x = x_ref[...]
    h = _modulated(x, vec_ref).astype(BF16)
    o = jnp.dot(h, woi_ref[...], preferred_element_type=F32)
    hs = hf_ref[...] + hb_ref[...]
    parts = []
    for hd in range(ML_HEADS):
        seg = hs[:, hd * ML_DV:(hd + 1) * ML_DV]
        parts.append(_head_rms(seg, ng_ref[0:1, hd * ML_DV:(hd + 1) * ML_DV]))
    hn = jnp.concatenate(parts, axis=1)
    y = jnp.dot((_sigmoid(o) * hn).astype(BF16), wo_ref[...], preferred_element_type=F32)
    o_ref[...] = x + vec_ref[2:3, :] * y


def _ml_out_call(x, hf, hb, vec, ng8, woi, wo, tm):
    t, d = x.shape
    row = lambda i: (i, 0)
    return pl.pallas_call(
        _ml_out_kernel,
        grid=(t // tm,),
        in_specs=[pl.BlockSpec((tm, d), row), pl.BlockSpec((tm, d), row), pl.BlockSpec((tm, d), row),
                  _full((8, d)), _full((8, d)), _full(woi.shape), _full(wo.shape)],
        out_specs=pl.BlockSpec((tm, d), row),
        out_shape=jax.ShapeDtypeStruct((t, d), F32),
        compiler_params=_cparams("arbitrary"),
        name="mlstm_out",
    )(x, hf, hb, vec, ng8, woi, wo)


def _pad8(rows):
    a = jnp.stack(rows)
    return jnp.concatenate([a, jnp.zeros((8 - a.shape[0], a.shape[1]), a.dtype)], axis=0)


def _rope_tables(n_tok):
    rows = n_tok // GRID_W
    row = jnp.repeat(jnp.arange(rows), GRID_W).astype(F32)
    col = jnp.tile(jnp.arange(GRID_W), rows).astype(F32)
    seg = HEAD_DIM // 2
    inv = ROPE_THETA ** (-jnp.arange(seg // 2, dtype=F32) / (seg // 2))
    ang_r = row[:, None] * inv
    ang_c = col[:, None] * inv
    ang = jnp.concatenate([ang_r, ang_r, ang_c, ang_c], axis=-1)
    return jnp.cos(ang), jnp.sin(ang)


def _ctx_read_at_or_after(i):
    return any((j % N_MIXERS) != 0 for j in range(i, DEPTH))


def kernel(x, c, ctx, c_ctx, mod_w, mod_b, norm_g, ffn_w13, ffn_w2, conv_w_in, conv_k, conv_w_out,
           attn_w_qkv, attn_q_g, attn_k_g, attn_w_o, mlstm_w_in, mlstm_b_gate, mlstm_norm_g, mlstm_w_o):
    d = D_MODEL
    t_lat = x.shape[1]
    t_ctx = ctx.shape[1]
    tm = 512
    tmc = t_ctx
    lat = x[0]
    cx = ctx[0]

    cvec = _pad8([c[0], c_ctx])
    modall = _mod_call(cvec, mod_w, mod_b)

    def vec(i, a, stream):
        m = modall[i, stream]
        base = 3 * a * d
        return _pad8([m[base:base + d], m[base + d:base + 2 * d], m[base + 2 * d:base + 3 * d], norm_g[i, a]])

    cos, sin = _rope_tables(t_lat)
    counters = [0, 0, 0]
    for i in range(DEPTH):
        kind = i % N_MIXERS
        j = counters[kind]
        counters[kind] += 1
        ctx_out = _ctx_read_at_or_after(i + 1)
        if not _ctx_read_at_or_after(i):
            cx = None
        w13a, w2a = ffn_w13[i, 0].astype(BF16), ffn_w2[i, 0].astype(BF16)
        w13b, w2b = ffn_w13[i, 1].astype(BF16), ffn_w2[i, 1].astype(BF16)
        lat = _ffn_call(lat, vec(i, 0, 0), w13a, w2a, tm)
        if cx is not None:
            cx = _ffn_call(cx, vec(i, 0, 1), w13a, w2a, tmc)
        vl, vc = vec(i, 1, 0), vec(i, 1, 1)
        if kind == 0:
            w_in, w_out = conv_w_in[j].astype(BF16), conv_w_out[j].astype(BF16)
            k8 = _pad8([conv_k[j, 0], conv_k[j, 1], conv_k[j, 2]])
            lat = _conv_call(lat, vl, w_in, k8, w_out, tm)
            if ctx_out:
                cx = _conv_call(cx, vc, w_in, k8, w_out, tmc)
        elif kind == 1:
            w_qkv, w_o = attn_w_qkv[j].astype(BF16), attn_w_o[j].astype(BF16)
            g8 = _pad8([attn_q_g[j], attn_k_g[j]])
            q, k, v = _qkv_call(lat, vl, w_qkv, g8, cos, sin, tm)
            qc, kc, vc_ = _qkv_call(cx, vc, w_qkv, g8, jnp.ones((t_ctx, HEAD_DIM), F32),
                                    jnp.zeros((t_ctx, HEAD_DIM), F32), tmc)
            att = _flash_call(q, [(kc, vc_), (k, v)], 256, (t_ctx, 512))
            lat = _proj_res_call(lat, att, vl, w_o, tm)
            if ctx_out:
                attc = _flash_call(qc, [(kc, vc_)], t_ctx, (t_ctx,))
                cx = _proj_res_call(cx, attc, vc, w_o, tmc)
        else:
            w_in = mlstm_w_in[j]
            kd = ML_KD
            g0 = kd + d
            q0 = g0 + ML_GATES
            o0 = q0 + kd
            wkt = w_in[:, :kd].T.astype(BF16)
            wv = w_in[:, kd:g0].astype(BF16)
            wg = w_in[:, g0:q0].astype(BF16)
            wq = w_in[:, q0:o0].astype(BF16)
            woi = w_in[:, o0:].astype(BF16)
            bg = mlstm_b_gate[j].reshape(1, ML_GATES)
            pc = _ml_proj_call(cx, vc, wq, wkt, wv, wg, wg.T, bg, bg.T, tmc)
            pl_ = _ml_proj_call(lat, vl, wq, wkt, wv, wg, wg.T, bg, bg.T, tm)
            c0 = jnp.zeros((2 * ML_HEADS, ML_DK, ML_STATE_W), F32)
            m0 = jnp.zeros((2 * ML_HEADS, 128), F32)
            hfc, hbc, c1, m1 = _ml_scan_call(*pc, c0, m0)
            hf, hb, _, _ = _ml_scan_call(*pl_, c1, m1)
            ng8 = _pad8([mlstm_norm_g[j]])
            wo = mlstm_w_o[j].astype(BF16)
            lat = _ml_out_call(lat, hf, hb, vl, ng8, woi, wo, tm)
            if ctx_out:
                cx = _ml_out_call(cx, hfc, hbc, vc, ng8, woi, wo, tmc)
        if ctx_out:
            cx = _ffn_call(cx, vec(i, 2, 1), w13b, w2b, tmc)
        else:
            cx = None
        lat = _ffn_call(lat, vec(i, 2, 0), w13b, w2b, tm)
    return lat[None]
```

```python
import functools

import jax
import jax.numpy as jnp
from jax import lax
from jax.experimental import pallas as pl
from jax.experimental.pallas import tpu as pltpu

F32 = jnp.float32
BF16 = jnp.bfloat16

D_MODEL = 1024
DEPTH = 4
N_MIXERS = 3
NORM_EPS = 1e-6
FFN_HIDDEN = 2816
FFN_CHUNK = 256
GRID_W = 64
HEAD_DIM = 128
Q_HEADS = 8
KV_HEADS = 2
GROUP = Q_HEADS // KV_HEADS
ROPE_THETA = 10000.0
ML_HEADS = 4
ML_DV = 256
ML_DK = 128
ML_KD = ML_HEADS * ML_DK
ML_GATES = 4 * ML_HEADS
ML_CHUNK = 128
ML_STATE_W = ML_DV + 128
QK_SCALE_LOG2 = (HEAD_DIM ** -0.5) * 1.4426950408889634

VMEM_LIMIT = 56 * 1024 * 1024


def _cparams(*sem):
    return pltpu.CompilerParams(dimension_semantics=sem, vmem_limit_bytes=VMEM_LIMIT)


def _full(shape):
    return pl.BlockSpec(shape, lambda *_: (0,) * len(shape))


def _modulated(x, vec_ref):
    ms = jnp.mean(x * x, axis=-1, keepdims=True)
    y = x * lax.rsqrt(ms + NORM_EPS) * vec_ref[3:4, :]
    return y * (1.0 + vec_ref[1:2, :]) + vec_ref[0:1, :]


def _sigmoid(x):
    return 1.0 / (1.0 + jnp.exp(-x))


def _log_sigmoid(x):
    return jnp.minimum(x, 0.0) - jnp.log(1.0 + jnp.exp(-jnp.abs(x)))


def _mod_kernel(c_ref, w_ref, b_ref, o_ref):
    c = c_ref[...]
    s = (c * _sigmoid(c)).astype(BF16)
    o_ref[0] = jnp.dot(s, w_ref[0].astype(BF16), preferred_element_type=F32) + b_ref[0]


def _mod_call(cvec, mod_w, mod_b):
    depth, d, n = mod_w.shape
    tn = 1024
    return pl.pallas_call(
        _mod_kernel,
        grid=(depth, n // tn),
        in_specs=[_full((8, d)),
                  pl.BlockSpec((1, d, tn), lambda l, j: (l, 0, j)),
                  pl.BlockSpec((1, 1, tn), lambda l, j: (l, 0, j))],
        out_specs=pl.BlockSpec((1, 8, tn), lambda l, j: (l, 0, j)),
        out_shape=jax.ShapeDtypeStruct((depth, 8, n), F32),
        compiler_params=_cparams("arbitrary", "arbitrary"),
        name="adaln_mod",
    )(cvec, mod_w, mod_b.reshape(depth, 1, n))


def _ffn_kernel(x_ref, vec_ref, w13_ref, w2_ref, o_ref, acc_ref):
    x = x_ref[...]
    h = _modulated(x, vec_ref).astype(BF16)
    for j in range(FFN_HIDDEN // FFN_CHUNK):
        lo = j * FFN_CHUNK
        a = jnp.dot(h, w13_ref[:, lo:lo + FFN_CHUNK], preferred_element_type=F32)
        b = jnp.dot(h, w13_ref[:, FFN_HIDDEN + lo:FFN_HIDDEN + lo + FFN_CHUNK], preferred_element_type=F32)
        u = (a * _sigmoid(a) * b).astype(BF16)
        y = jnp.dot(u, w2_ref[lo:lo + FFN_CHUNK, :], preferred_element_type=F32)
        if j == 0:
            acc_ref[...] = y
        else:
            acc_ref[...] += y
    o_ref[...] = x + (0.5 * vec_ref[2:3, :]) * acc_ref[...]


def _ffn_call(x, vec, w13, w2, tm):
    t, d = x.shape
    return pl.pallas_call(
        _ffn_kernel,
        grid=(t // tm,),
        in_specs=[pl.BlockSpec((tm, d), lambda i: (i, 0)),
                  _full((8, d)),
                  _full(w13.shape),
                  _full(w2.shape)],
        out_specs=pl.BlockSpec((tm, d), lambda i: (i, 0)),
        out_shape=jax.ShapeDtypeStruct((t, d), F32),
        scratch_shapes=[pltpu.VMEM((tm, d), F32)],
        compiler_params=_cparams("arbitrary"),
        name="swiglu_half_step",
    )(x, vec, w13, w2)


def _proj_res_kernel(x_ref, a_ref, vec_ref, w_ref, o_ref):
    y = jnp.dot(a_ref[...], w_ref[...], preferred_element_type=F32)
    o_ref[...] = x_ref[...] + vec_ref[2:3, :] * y


def _proj_res_call(x, a, vec, w, tm):
    t, d = x.shape
    return pl.pallas_call(
        _proj_res_kernel,
        grid=(t // tm,),
        in_specs=[pl.BlockSpec((tm, d), lambda i: (i, 0)),
                  pl.BlockSpec((tm, a.shape[1]), lambda i: (i, 0)),
                  _full((8, d)),
                  _full(w.shape)],
        out_specs=pl.BlockSpec((tm, d), lambda i: (i, 0)),
        out_shape=jax.ShapeDtypeStruct((t, d), F32),
        compiler_params=_cparams("arbitrary"),
        name="out_proj_residual",
    )(x, a, vec, w)


CONV_HALO = 16


def _conv_kernel(x_ref, xp_ref, xn_ref, vec_ref, win_ref, k_ref, wout_ref, o_ref, h_ref, acc_ref, *, tm):
    i = pl.program_id(0)
    last = pl.num_programs(0) - 1
    x = x_ref[...]
    d = x.shape[1]
    keep_prev = (i > 0).astype(F32)
    keep_next = (i < last).astype(F32)
    h_ref[0:CONV_HALO, :] = (_modulated(xp_ref[...], vec_ref) * keep_prev).astype(BF16)
    h_ref[CONV_HALO:CONV_HALO + tm, :] = _modulated(x, vec_ref).astype(BF16)
    h_ref[CONV_HALO + tm:, :] = (_modulated(xn_ref[...], vec_ref) * keep_next).astype(BF16)
    rows = tm + 2 * CONV_HALO
    cw = 256
    for j in range(d // cw):
        lo = j * cw
        hall = h_ref[...]
        cg = jnp.dot(hall, win_ref[:, d + lo:d + lo + cw], preferred_element_type=F32)
        xv = jnp.dot(hall, win_ref[:, 2 * d + lo:2 * d + lo + cw], preferred_element_type=F32)
        u = cg * xv
        up = pltpu.roll(u, 1, axis=0)
        un = pltpu.roll(u, rows - 1, axis=0)
        kk = k_ref[:, lo:lo + cw]
        conv = kk[0:1, :] * up + kk[1:2, :] * u + kk[2:3, :] * un
        conv = conv[CONV_HALO:CONV_HALO + tm, :]
        bg = jnp.dot(h_ref[CONV_HALO:CONV_HALO + tm, :], win_ref[:, lo:lo + cw], preferred_element_type=F32)
        y = jnp.dot((bg * conv).astype(BF16), wout_ref[lo:lo + cw, :], preferred_element_type=F32)
        if j == 0:
            acc_ref[...] = y
        else:
            acc_ref[...] += y
    o_ref[...] = x + vec_ref[2:3, :] * acc_ref[...]


def _conv_call(x, vec, w_in, k8, w_out, tm):
    t, d = x.shape
    hb = tm // CONV_HALO
    nhb = t // CONV_HALO
    return pl.pallas_call(
        functools.partial(_conv_kernel, tm=tm),
        grid=(t // tm,),
        in_specs=[pl.BlockSpec((tm, d), lambda i: (i, 0)),
                  pl.BlockSpec((CONV_HALO, d), lambda i: (jnp.maximum(i * hb - 1, 0), 0)),
                  pl.BlockSpec((CONV_HALO, d), lambda i: (jnp.minimum((i + 1) * hb, nhb - 1), 0)),
                  _full((8, d)),
                  _full(w_in.shape),
                  _full((8, d)),
                  _full(w_out.shape)],
        out_specs=pl.BlockSpec((tm, d), lambda i: (i, 0)),
        out_shape=jax.ShapeDtypeStruct((t, d), F32),
        scratch_shapes=[pltpu.VMEM((tm + 2 * CONV_HALO, d), BF16), pltpu.VMEM((tm, d), F32)],
        compiler_params=_cparams("arbitrary"),
        name="short_conv_mixer",
    )(x, x, x, vec, w_in, k8, w_out)


def _head_rms(t, g):
    return t * lax.rsqrt(jnp.mean(t * t, axis=-1, keepdims=True) + NORM_EPS) * g


def _rope(t, cos, sin):
    lane = lax.broadcasted_iota(jnp.int32, t.shape, 1)
    first = (lane % (HEAD_DIM // 2)) < (HEAD_DIM // 4)
    rot = jnp.where(first, -pltpu.roll(t, HEAD_DIM - HEAD_DIM // 4, axis=1), pltpu.roll(t, HEAD_DIM // 4, axis=1))
    return t * cos + rot * sin


ATT_TQ = 256
ATT_VROWS = HEAD_DIM + 16


def _qkv_kernel(x_ref, vec_ref, w_ref, g_ref, cos_ref, sin_ref, q_ref, k_ref, v_ref):
    h = _modulated(x_ref[...], vec_ref).astype(BF16)
    p = jnp.dot(h, w_ref[...], preferred_element_type=F32)
    cos = cos_ref[...]
    sin = sin_ref[...]
    tm = h.shape[0]
    for hq in range(Q_HEADS):
        t = _head_rms(p[:, hq * HEAD_DIM:(hq + 1) * HEAD_DIM], g_ref[0:1, :])
        qt = jnp.transpose(_rope(t, cos, sin) * QK_SCALE_LOG2).astype(BF16)
        g = hq % GROUP
        for sub in range(tm // ATT_TQ):
            q_ref[hq // GROUP, sub, :, g * ATT_TQ:(g + 1) * ATT_TQ] = qt[:, sub * ATT_TQ:(sub + 1) * ATT_TQ]
    qd = Q_HEADS * HEAD_DIM
    kvd = KV_HEADS * HEAD_DIM
    for hk in range(KV_HEADS):
        t = _head_rms(p[:, qd + hk * HEAD_DIM:qd + (hk + 1) * HEAD_DIM], g_ref[1:2, :])
        k_ref[hk] = _rope(t, cos, sin).astype(BF16)
        v = p[:, qd + kvd + hk * HEAD_DIM:qd + kvd + (hk + 1) * HEAD_DIM]
        v_ref[hk, 0, 0:HEAD_DIM, :] = jnp.transpose(v).astype(BF16)
        v_ref[hk, 0, HEAD_DIM:, :] = jnp.ones((ATT_VROWS - HEAD_DIM, tm), BF16)


def _qkv_call(x, vec, w_qkv, g8, cos, sin, tm):
    t, d = x.shape
    nsub = tm // ATT_TQ
    return pl.pallas_call(
        _qkv_kernel,
        grid=(t // tm,),
        in_specs=[pl.BlockSpec((tm, d), lambda i: (i, 0)),
                  _full((8, d)),
                  _full(w_qkv.shape),
                  _full((8, HEAD_DIM)),
                  pl.BlockSpec((tm, HEAD_DIM), lambda i: (i, 0)),
                  pl.BlockSpec((tm, HEAD_DIM), lambda i: (i, 0))],
        out_specs=[pl.BlockSpec((KV_HEADS, nsub, HEAD_DIM, GROUP * ATT_TQ), lambda i: (0, i, 0, 0)),
                   pl.BlockSpec((KV_HEADS, tm, HEAD_DIM), lambda i: (0, i, 0)),
                   pl.BlockSpec((KV_HEADS, 1, ATT_VROWS, tm), lambda i: (0, i, 0, 0))],
        out_shape=[jax.ShapeDtypeStruct((KV_HEADS, t // ATT_TQ, HEAD_DIM, GROUP * ATT_TQ), BF16),
                   jax.ShapeDtypeStruct((KV_HEADS, t, HEAD_DIM), BF16),
                   jax.ShapeDtypeStruct((KV_HEADS, t // tm, ATT_VROWS, tm), BF16)],
        compiler_params=_cparams("arbitrary"),
        name="gqa_qkv_proj",
    )(x, vec, w_qkv, g8, cos, sin)


def _flash_kernel(*refs, tk, nk):
    if nk:
        q_ref, kc_ref, vc_ref, k_ref, v_ref, o_ref, m_ref, acc_ref, sa_ref, sb_ref = refs
    else:
        q_ref, kc_ref, vc_ref, o_ref, m_ref, acc_ref = refs
    qt = q_ref[0, 0]

    def scores(ks):
        return jnp.dot(ks, qt, preferred_element_type=F32)

    def absorb(st, vt):
        m_old = m_ref[...]
        m_new = jnp.maximum(m_old, jnp.max(st, axis=0, keepdims=True))
        alpha = jnp.exp2(m_old - m_new)
        pt = jnp.exp2(st - m_new).astype(BF16)
        acc_ref[...] = alpha * acc_ref[...] + jnp.dot(vt, pt, preferred_element_type=F32)
        m_ref[...] = m_new

    st = scores(kc_ref[0])
    m0 = jnp.max(st, axis=0, keepdims=True)
    acc_ref[...] = jnp.dot(vc_ref[0, 0], jnp.exp2(st - m0).astype(BF16), preferred_element_type=F32)
    m_ref[...] = m0

    if nk:
        def key_tile(j):
            return k_ref[0, pl.ds(pl.multiple_of(j * tk, tk), tk), :]

        sa_ref[...] = scores(key_tile(0))

        def body(jj, carry):
            j = 2 * jj
            sb_ref[...] = scores(key_tile(j + 1))
            absorb(sa_ref[...], v_ref[0, j])
            sa_ref[...] = scores(key_tile(j + 2))
            absorb(sb_ref[...], v_ref[0, j + 1])
            return carry

        lax.fori_loop(0, nk // 2 - 1, body, 0)
        sb_ref[...] = scores(key_tile(nk - 1))
        absorb(sa_ref[...], v_ref[0, nk - 2])
        absorb(sb_ref[...], v_ref[0, nk - 1])

    acc = acc_ref[...]
    out = acc[0:HEAD_DIM, :] / acc[HEAD_DIM:HEAD_DIM + 1, :]
    for g in range(GROUP):
        o_ref[:, g * HEAD_DIM:(g + 1) * HEAD_DIM] = jnp.transpose(
            out[:, g * ATT_TQ:(g + 1) * ATT_TQ]).astype(o_ref.dtype)


def _flash_call(q, kc, vc, k, v):
    nq = q.shape[1]
    cols = GROUP * ATT_TQ
    head3 = lambda h, i: (h, 0, 0)
    head4 = lambda h, i: (h, 0, 0, 0)
    in_specs = [pl.BlockSpec((1, 1, HEAD_DIM, cols), lambda h, i: (h, i, 0, 0)),
                pl.BlockSpec((1,) + kc.shape[1:], head3), pl.BlockSpec((1,) + vc.shape[1:], head4)]
    args = [q, kc, vc]
    scratch = [pltpu.VMEM((1, cols), F32), pltpu.VMEM((ATT_VROWS, cols), F32)]
    nk = tk = 0
    if k is not None:
        nk, tk = v.shape[1], v.shape[3]
        in_specs += [pl.BlockSpec((1,) + k.shape[1:], head3), pl.BlockSpec((1,) + v.shape[1:], head4)]
        args += [k, v]
        assert nk % 2 == 0 and nk >= 2
        scratch += [pltpu.VMEM((tk, cols), F32), pltpu.VMEM((tk, cols), F32)]
    return pl.pallas_call(
        functools.partial(_flash_kernel, tk=tk, nk=nk),
        grid=(KV_HEADS, nq),
        in_specs=in_specs,
        out_specs=pl.BlockSpec((ATT_TQ, GROUP * HEAD_DIM), lambda h, i: (i, h)),
        out_shape=jax.ShapeDtypeStruct((nq * ATT_TQ, Q_HEADS * HEAD_DIM), BF16),
        scratch_shapes=scratch,
        compiler_params=_cparams("arbitrary", "arbitrary"),
        name="gqa_flash",
    )(*args)


def _ml_proj_kernel(x_ref, vec_ref, wq_ref, wkt_ref, wv_ref, wg_ref, wgt_ref, bg_ref, bgt_ref,
                    q_ref, kt_ref, v_ref, gc_ref, gr_ref):
    h = _modulated(x_ref[...], vec_ref).astype(BF16)
    nt = (((1,), (1,)), ((), ()))
    q_ref[...] = jnp.dot(h, wq_ref[...], preferred_element_type=F32).astype(BF16)
    kt = lax.dot_general(wkt_ref[...], h, nt, preferred_element_type=F32)
    kt_ref[...] = (kt * (ML_DK ** -0.5)).astype(BF16)
    v_ref[...] = jnp.dot(h, wv_ref[...], preferred_element_type=F32).astype(BF16)
    gc_ref[...] = jnp.dot(h, wg_ref[...], preferred_element_type=F32) + bg_ref[...]
    gr_ref[...] = lax.dot_general(wgt_ref[...], h, nt, preferred_element_type=F32) + bgt_ref[...]


def _ml_proj_call(x, vec, wq, wkt, wv, wg, wgt, bg, bgt, tm):
    t, d = x.shape
    row = lambda i: (i, 0)
    col = lambda i: (0, i)
    return pl.pallas_call(
        _ml_proj_kernel,
        grid=(t // tm,),
        in_specs=[pl.BlockSpec((tm, d), row), _full((8, d)), _full(wq.shape), _full(wkt.shape), _full(wv.shape),
                  _full(wg.shape), _full(wgt.shape), _full(bg.shape), _full(bgt.shape)],
        out_specs=[pl.BlockSpec((tm, ML_KD), row), pl.BlockSpec((ML_KD, tm), col), pl.BlockSpec((tm, d), row),
                   pl.BlockSpec((tm, ML_GATES), row), pl.BlockSpec((ML_GATES, tm), col)],
        out_shape=[jax.ShapeDtypeStruct((t, ML_KD), BF16), jax.ShapeDtypeStruct((ML_KD, t), BF16),
                   jax.ShapeDtypeStruct((t, d), BF16), jax.ShapeDtypeStruct((t, ML_GATES), F32),
                   jax.ShapeDtypeStruct((ML_GATES, t), F32)],
        compiler_params=_cparams("arbitrary"),
        name="mlstm_in_proj",
    )(x, vec, wq, wkt, wv, wg, wgt, bg, bgt)


def _ml_scan_kernel(qf_ref, ktf_ref, vf_ref, gcf_ref, grf_ref, qb_ref, ktb_ref, vb_ref, gcb_ref, grb_ref,
                    c0_ref, m0_ref, hf_ref, hb_ref, cout_ref, mout_ref, c_ref, m_ref):
    n = pl.program_id(0)
    L = ML_CHUNK

    @pl.when(n == 0)
    def _():
        c_ref[...] = c0_ref[...]
        m_ref[...] = m0_ref[...]

    r = lax.broadcasted_iota(jnp.int32, (L, L), 0)
    s_ = lax.broadcasted_iota(jnp.int32, (L, L), 1)
    lower = r >= s_
    upper = r <= s_
    ones_blk = jnp.ones((L, ML_STATE_W - ML_DV), BF16)
    hi = lax.Precision.HIGHEST
    per_dir = ((qf_ref, ktf_ref, vf_ref, gcf_ref, grf_ref, hf_ref, lower, upper, L - 1),
               (qb_ref, ktb_ref, vb_ref, gcb_ref, grb_ref, hb_ref, upper, lower, 0))
    for d, (q_ref, kt_ref, v_ref, gc_ref, gr_ref, h_ref, seen, seen_t, end) in enumerate(per_dir):
        gcol = gc_ref[...]
        grow = gr_ref[...]
        a_col = jnp.dot(seen.astype(F32), _log_sigmoid(gcol), precision=hi, preferred_element_type=F32)
        a_row = jnp.dot(_log_sigmoid(grow), seen_t.astype(F32), precision=hi, preferred_element_type=F32)
        for hd in range(ML_HEADS):
            ji = (2 * d) * ML_HEADS + hd
            jf = (2 * d + 1) * ML_HEADS + hd
            st = d * ML_HEADS + hd
            a_c = a_col[:, jf:jf + 1]
            a_r = a_row[jf:jf + 1, :]
            b_r = grow[ji:ji + 1, :] - a_r
            a_end = a_r[:, end:end + 1]
            m_prev = m_ref[st:st + 1, 0:1]
            dm = jnp.where(seen, a_c + b_r, -jnp.inf)
            inter = a_c + m_prev
            m_t = jnp.maximum(jnp.max(dm, axis=1, keepdims=True), inter)
            w = jnp.exp(dm - m_t)
            qh = q_ref[:, hd * ML_DK:(hd + 1) * ML_DK]
            kth = kt_ref[hd * ML_DK:(hd + 1) * ML_DK, :]
            sw = jnp.dot(qh, kth, preferred_element_type=F32) * w
            sc = jnp.exp(inter - m_t)
            vp = jnp.concatenate([v_ref[:, hd * ML_DV:(hd + 1) * ML_DV], ones_blk], axis=1)
            cst = c_ref[st]
            hx = (jnp.dot(sw.astype(BF16), vp, preferred_element_type=F32)
                  + sc * jnp.dot(qh, cst.astype(BF16), preferred_element_type=F32))
            den = jnp.maximum(jnp.abs(hx[:, ML_DV:]), jnp.exp(-m_t))
            inv = 1.0 / den
            h_ref[:, hd * ML_DV:hd * ML_DV + 128] = hx[:, 0:128] * inv
            h_ref[:, hd * ML_DV + 128:(hd + 1) * ML_DV] = hx[:, 128:ML_DV] * inv
            w_end = a_end + b_r
            m_new = jnp.maximum(a_end + m_prev, jnp.max(w_end, axis=1, keepdims=True))
            e_r = jnp.exp(w_end - m_new)
            decay = jnp.exp(a_end + m_prev - m_new)
            ke = (kth.astype(F32) * e_r).astype(BF16)
            c_ref[st] = decay * cst + jnp.dot(ke, vp, preferred_element_type=F32)
            m_ref[st:st + 1, :] = jnp.broadcast_to(m_new, (1, 128))

    @pl.when(n == pl.num_programs(0) - 1)
    def _():
        cout_ref[...] = c_ref[...]
        mout_ref[...] = m_ref[...]


def _ml_scan_call(q, kt, v, gc, gr, c0, m0):
    t, d = v.shape
    L = ML_CHUNK
    nc = t // L
    fr = lambda n: (n, 0)
    fc = lambda n: (0, n)
    br = lambda n: (nc - 1 - n, 0)
    bc = lambda n: (0, nc - 1 - n)

    def specs(rm, cm):
        return [pl.BlockSpec((L, ML_KD), rm), pl.BlockSpec((ML_KD, L), cm), pl.BlockSpec((L, d), rm),
                pl.BlockSpec((L, ML_GATES), rm), pl.BlockSpec((ML_GATES, L), cm)]

    nst = 2 * ML_HEADS
    return pl.pallas_call(
        _ml_scan_kernel,
        grid=(nc,),
        in_specs=specs(fr, fc) + specs(br, bc) + [_full(c0.shape), _full(m0.shape)],
        out_specs=[pl.BlockSpec((L, d), fr), pl.BlockSpec((L, d), br), _full(c0.shape), _full(m0.shape)],
        out_shape=[jax.ShapeDtypeStruct((t, d), F32), jax.ShapeDtypeStruct((t, d), F32),
                   jax.ShapeDtypeStruct(c0.shape, F32), jax.ShapeDtypeStruct(m0.shape, F32)],
        scratch_shapes=[pltpu.VMEM((nst, ML_DK, ML_STATE_W), F32), pltpu.VMEM((nst, 128), F32)],
        compiler_params=_cparams("arbitrary"),
        name="mlstm_scan",
    )(q, kt, v, gc, gr, q, kt, v, gc, gr, c0, m0)


def _ml_out_kernel(x_ref, hf_ref, hb_ref, vec_ref, ng_ref, woi_ref, wo_ref, o_ref):
    x = x_ref[...]
    h = _modulated(x, vec_ref).astype(BF16)
    o = jnp.dot(h, woi_ref[...], preferred_element_type=F32)
    hs = hf_ref[...] + hb_ref[...]
    parts = []
    for hd in range(ML_HEADS):
        seg = hs[:, hd * ML_DV:(hd + 1) * ML_DV]
        parts.append(_head_rms(seg, ng_ref[0:1, hd * ML_DV:(hd + 1) * ML_DV]))
    hn = jnp.concatenate(parts, axis=1)
    y = jnp.dot((_sigmoid(o) * hn).astype(BF16), wo_ref[...], preferred_element_type=F32)
    o_ref[...] = x + vec_ref[2:3, :] * y


def _ml_out_call(x, hf, hb, vec, ng8, woi, wo, tm):
    t, d = x.shape
    row = lambda i: (i, 0)
    return pl.pallas_call(
        _ml_out_kernel,
        grid=(t // tm,),
        in_specs=[pl.BlockSpec((tm, d), row), pl.BlockSpec((tm, d), row), pl.BlockSpec((tm, d), row),
                  _full((8, d)), _full((8, d)), _full(woi.shape), _full(wo.shape)],
        out_specs=pl.BlockSpec((tm, d), row),
        out_shape=jax.ShapeDtypeStruct((t, d), F32),
        compiler_params=_cparams("arbitrary"),
        name="mlstm_out",
    )(x, hf, hb, vec, ng8, woi, wo)


def _pad8(rows):
    a = jnp.stack(rows)
    return jnp.concatenate([a, jnp.zeros((8 - a.shape[0], a.shape[1]), a.dtype)], axis=0)


def _rope_tables(n_tok):
    rows = n_tok // GRID_W
    row = jnp.repeat(jnp.arange(rows), GRID_W).astype(F32)
    col = jnp.tile(jnp.arange(GRID_W), rows).astype(F32)
    seg = HEAD_DIM // 2
    inv = ROPE_THETA ** (-jnp.arange(seg // 2, dtype=F32) / (seg // 2))
    ang_r = row[:, None] * inv
    ang_c = col[:, None] * inv
    ang = jnp.concatenate([ang_r, ang_r, ang_c, ang_c], axis=-1)
    return jnp.cos(ang), jnp.sin(ang)


def _ctx_read_at_or_after(i):
    return any((j % N_MIXERS) != 0 for j in range(i, DEPTH))


def kernel(x, c, ctx, c_ctx, mod_w, mod_b, norm_g, ffn_w13, ffn_w2, conv_w_in, conv_k, conv_w_out,
           attn_w_qkv, attn_q_g, attn_k_g, attn_w_o, mlstm_w_in, mlstm_b_gate, mlstm_norm_g, mlstm_w_o):
    d = D_MODEL
    t_lat = x.shape[1]
    t_ctx = ctx.shape[1]
    tm = 512
    tmc = t_ctx
    lat = x[0]
    cx = ctx[0]

    cvec = _pad8([c[0], c_ctx])
    modall = _mod_call(cvec, mod_w, mod_b)

    def vec(i, a, stream):
        m = modall[i, stream]
        base = 3 * a * d
        return _pad8([m[base:base + d], m[base + d:base + 2 * d], m[base + 2 * d:base + 3 * d], norm_g[i, a]])

    cos, sin = _rope_tables(t_lat)
    counters = [0, 0, 0]
    for i in range(DEPTH):
        kind = i % N_MIXERS
        j = counters[kind]
        counters[kind] += 1
        ctx_out = _ctx_read_at_or_after(i + 1)
        if not _ctx_read_at_or_after(i):
            cx = None
        w13a, w2a = ffn_w13[i, 0].astype(BF16), ffn_w2[i, 0].astype(BF16)
        w13b, w2b = ffn_w13[i, 1].astype(BF16), ffn_w2[i, 1].astype(BF16)
        lat = _ffn_call(lat, vec(i, 0, 0), w13a, w2a, tm)
        if cx is not None:
            cx = _ffn_call(cx, vec(i, 0, 1), w13a, w2a, tmc)
        vl, vc = vec(i, 1, 0), vec(i, 1, 1)
        if kind == 0:
            w_in, w_out = conv_w_in[j].astype(BF16), conv_w_out[j].astype(BF16)
            k8 = _pad8([conv_k[j, 0], conv_k[j, 1], conv_k[j, 2]])
            lat = _conv_call(lat, vl, w_in, k8, w_out, tm)
            if ctx_out:
                cx = _conv_call(cx, vc, w_in, k8, w_out, tmc)
        elif kind == 1:
            w_qkv, w_o = attn_w_qkv[j].astype(BF16), attn_w_o[j].astype(BF16)
            g8 = _pad8([attn_q_g[j], attn_k_g[j]])
            q, k, v = _qkv_call(lat, vl, w_qkv, g8, cos, sin, tm)
            qc, kc, vc_ = _qkv_call(cx, vc, w_qkv, g8, jnp.ones((t_ctx, HEAD_DIM), F32),
                                    jnp.zeros((t_ctx, HEAD_DIM), F32), tmc)
            att = _flash_call(q, kc, vc_, k, v)
            lat = _proj_res_call(lat, att, vl, w_o, tm)
            if ctx_out:
                attc = _flash_call(qc, kc, vc_, None, None)
                cx = _proj_res_call(cx, attc, vc, w_o, tmc)
        else:
            w_in = mlstm_w_in[j]
            kd = ML_KD
            g0 = kd + d
            q0 = g0 + ML_GATES
            o0 = q0 + kd
            wkt = w_in[:, :kd].T.astype(BF16)
            wv = w_in[:, kd:g0].astype(BF16)
            wg = w_in[:, g0:q0].astype(BF16)
            wq = w_in[:, q0:o0].astype(BF16)
            woi = w_in[:, o0:].astype(BF16)
            bg = mlstm_b_gate[j].reshape(1, ML_GATES)
            pc = _ml_proj_call(cx, vc, wq, wkt, wv, wg, wg.T, bg, bg.T, tmc)
            pl_ = _ml_proj_call(lat, vl, wq, wkt, wv, wg, wg.T, bg, bg.T, tm)
            c0 = jnp.zeros((2 * ML_HEADS, ML_DK, ML_STATE_W), F32)
            m0 = jnp.zeros((2 * ML_HEADS, 128), F32)
            hfc, hbc, c1, m1 = _ml_scan_call(*pc, c0, m0)
            hf, hb, _, _ = _ml_scan_call(*pl_, c1, m1)
            ng8 = _pad8([mlstm_norm_g[j]])
            wo = mlstm_w_o[j].astype(BF16)
            lat = _ml_out_call(lat, hf, hb, vl, ng8, woi, wo, tm)
            if ctx_out:
                cx = _ml_out_call(cx, hfc, hbc, vc, ng8, woi, wo, tmc)
        if ctx_out:
            cx = _ffn_call(cx, vec(i, 2, 1), w13b, w2b, tmc)
        else:
            cx = None
        lat = _ffn_call(lat, vec(i, 2, 0), w13b, w2b, tm)
    return lat[None]
```

```python
import functools

import jax
import jax.numpy as jnp
from jax import lax
from jax.experimental import pallas as pl
from jax.experimental.pallas import tpu as pltpu

F32 = jnp.float32
BF16 = jnp.bfloat16

D_MODEL = 1024
DEPTH = 4
N_MIXERS = 3
NORM_EPS = 1e-6
FFN_HIDDEN = 2816
FFN_CHUNK = 256
GRID_W = 64
HEAD_DIM = 128
Q_HEADS = 8
KV_HEADS = 2
GROUP = Q_HEADS // KV_HEADS
ROPE_THETA = 10000.0
ML_HEADS = 4
ML_DV = 256
ML_DK = 128
ML_KD = ML_HEADS * ML_DK
ML_GATES = 4 * ML_HEADS
ML_CHUNK = 256
ML_STATE_W = ML_DV + 128
QK_SCALE_LOG2 = (HEAD_DIM ** -0.5) * 1.4426950408889634

VMEM_LIMIT = 56 * 1024 * 1024


def _cparams(*sem):
    return pltpu.CompilerParams(dimension_semantics=sem, vmem_limit_bytes=VMEM_LIMIT)


def _full(shape):
    return pl.BlockSpec(shape, lambda *_: (0,) * len(shape))


def _modulated(x, vec_ref):
    ms = jnp.mean(x * x, axis=-1, keepdims=True)
    y = x * lax.rsqrt(ms + NORM_EPS) * vec_ref[3:4, :]
    return y * (1.0 + vec_ref[1:2, :]) + vec_ref[0:1, :]


def _sigmoid(x):
    return 1.0 / (1.0 + jnp.exp(-x))


def _log_sigmoid(x):
    return jnp.minimum(x, 0.0) - jnp.log(1.0 + jnp.exp(-jnp.abs(x)))


def _mod_kernel(c_ref, w_ref, b_ref, o_ref):
    c = c_ref[...]
    s = (c * _sigmoid(c)).astype(BF16)
    o_ref[0] = jnp.dot(s, w_ref[0].astype(BF16), preferred_element_type=F32) + b_ref[0]


def _mod_call(cvec, mod_w, mod_b):
    depth, d, n = mod_w.shape
    tn = 1024
    return pl.pallas_call(
        _mod_kernel,
        grid=(depth, n // tn),
        in_specs=[_full((8, d)),
                  pl.BlockSpec((1, d, tn), lambda l, j: (l, 0, j)),
                  pl.BlockSpec((1, 1, tn), lambda l, j: (l, 0, j))],
        out_specs=pl.BlockSpec((1, 8, tn), lambda l, j: (l, 0, j)),
        out_shape=jax.ShapeDtypeStruct((depth, 8, n), F32),
        compiler_params=_cparams("arbitrary", "arbitrary"),
        name="adaln_mod",
    )(cvec, mod_w, mod_b.reshape(depth, 1, n))


def _ffn_kernel(x_ref, vec_ref, w13_ref, w2_ref, o_ref, acc_ref):
    x = x_ref[...]
    h = _modulated(x, vec_ref).astype(BF16)
    for j in range(FFN_HIDDEN // FFN_CHUNK):
        lo = j * FFN_CHUNK
        a = jnp.dot(h, w13_ref[:, lo:lo + FFN_CHUNK], preferred_element_type=F32)
        b = jnp.dot(h, w13_ref[:, FFN_HIDDEN + lo:FFN_HIDDEN + lo + FFN_CHUNK], preferred_element_type=F32)
        u = (a * _sigmoid(a) * b).astype(BF16)
        y = jnp.dot(u, w2_ref[lo:lo + FFN_CHUNK, :], preferred_element_type=F32)
        if j == 0:
            acc_ref[...] = y
        else:
            acc_ref[...] += y
    o_ref[...] = x + (0.5 * vec_ref[2:3, :]) * acc_ref[...]


def _ffn_call(x, vec, w13, w2, tm):
    t, d = x.shape
    return pl.pallas_call(
        _ffn_kernel,
        grid=(t // tm,),
        in_specs=[pl.BlockSpec((tm, d), lambda i: (i, 0)),
                  _full((8, d)),
                  _full(w13.shape),
                  _full(w2.shape)],
        out_specs=pl.BlockSpec((tm, d), lambda i: (i, 0)),
        out_shape=jax.ShapeDtypeStruct((t, d), F32),
        scratch_shapes=[pltpu.VMEM((tm, d), F32)],
        compiler_params=_cparams("arbitrary"),
        name="swiglu_half_step",
    )(x, vec, w13, w2)


def _proj_res_kernel(x_ref, a_ref, vec_ref, w_ref, o_ref):
    y = jnp.dot(a_ref[...], w_ref[...], preferred_element_type=F32)
    o_ref[...] = x_ref[...] + vec_ref[2:3, :] * y


def _proj_res_call(x, a, vec, w, tm):
    t, d = x.shape
    return pl.pallas_call(
        _proj_res_kernel,
        grid=(t // tm,),
        in_specs=[pl.BlockSpec((tm, d), lambda i: (i, 0)),
                  pl.BlockSpec((tm, a.shape[1]), lambda i: (i, 0)),
                  _full((8, d)),
                  _full(w.shape)],
        out_specs=pl.BlockSpec((tm, d), lambda i: (i, 0)),
        out_shape=jax.ShapeDtypeStruct((t, d), F32),
        compiler_params=_cparams("arbitrary"),
        name="out_proj_residual",
    )(x, a, vec, w)


CONV_HALO = 16


def _conv_kernel(x_ref, xp_ref, xn_ref, vec_ref, win_ref, k_ref, wout_ref, o_ref, h_ref, acc_ref, *, tm):
    i = pl.program_id(0)
    last = pl.num_programs(0) - 1
    x = x_ref[...]
    d = x.shape[1]
    keep_prev = (i > 0).astype(F32)
    keep_next = (i < last).astype(F32)
    h_ref[0:CONV_HALO, :] = (_modulated(xp_ref[...], vec_ref) * keep_prev).astype(BF16)
    h_ref[CONV_HALO:CONV_HALO + tm, :] = _modulated(x, vec_ref).astype(BF16)
    h_ref[CONV_HALO + tm:, :] = (_modulated(xn_ref[...], vec_ref) * keep_next).astype(BF16)
    rows = tm + 2 * CONV_HALO
    cw = 256
    for j in range(d // cw):
        lo = j * cw
        hall = h_ref[...]
        cg = jnp.dot(hall, win_ref[:, d + lo:d + lo + cw], preferred_element_type=F32)
        xv = jnp.dot(hall, win_ref[:, 2 * d + lo:2 * d + lo + cw], preferred_element_type=F32)
        u = cg * xv
        up = pltpu.roll(u, 1, axis=0)
        un = pltpu.roll(u, rows - 1, axis=0)
        kk = k_ref[:, lo:lo + cw]
        conv = kk[0:1, :] * up + kk[1:2, :] * u + kk[2:3, :] * un
        conv = conv[CONV_HALO:CONV_HALO + tm, :]
        bg = jnp.dot(h_ref[CONV_HALO:CONV_HALO + tm, :], win_ref[:, lo:lo + cw], preferred_element_type=F32)
        y = jnp.dot((bg * conv).astype(BF16), wout_ref[lo:lo + cw, :], preferred_element_type=F32)
        if j == 0:
            acc_ref[...] = y
        else:
            acc_ref[...] += y
    o_ref[...] = x + vec_ref[2:3, :] * acc_ref[...]


def _conv_call(x, vec, w_in, k8, w_out, tm):
    t, d = x.shape
    hb = tm // CONV_HALO
    nhb = t // CONV_HALO
    return pl.pallas_call(
        functools.partial(_conv_kernel, tm=tm),
        grid=(t // tm,),
        in_specs=[pl.BlockSpec((tm, d), lambda i: (i, 0)),
                  pl.BlockSpec((CONV_HALO, d), lambda i: (jnp.maximum(i * hb - 1, 0), 0)),
                  pl.BlockSpec((CONV_HALO, d), lambda i: (jnp.minimum((i + 1) * hb, nhb - 1), 0)),
                  _full((8, d)),
                  _full(w_in.shape),
                  _full((8, d)),
                  _full(w_out.shape)],
        out_specs=pl.BlockSpec((tm, d), lambda i: (i, 0)),
        out_shape=jax.ShapeDtypeStruct((t, d), F32),
        scratch_shapes=[pltpu.VMEM((tm + 2 * CONV_HALO, d), BF16), pltpu.VMEM((tm, d), F32)],
        compiler_params=_cparams("arbitrary"),
        name="short_conv_mixer",
    )(x, x, x, vec, w_in, k8, w_out)


def _head_rms(t, g):
    return t * lax.rsqrt(jnp.mean(t * t, axis=-1, keepdims=True) + NORM_EPS) * g


def _rope(t, cos, sin):
    lane = lax.broadcasted_iota(jnp.int32, t.shape, 1)
    first = (lane % (HEAD_DIM // 2)) < (HEAD_DIM // 4)
    rot = jnp.where(first, -pltpu.roll(t, HEAD_DIM - HEAD_DIM // 4, axis=1), pltpu.roll(t, HEAD_DIM // 4, axis=1))
    return t * cos + rot * sin


ATT_TQ = 256
ATT_VROWS = HEAD_DIM + 16


def _qkv_kernel(x_ref, vec_ref, wqv_ref, wk_ref, gq_ref, gk_ref, cos_ref, sin_ref, cost_ref, sint_ref,
                q_ref, k_ref, v_ref, *, tq):
    h = _modulated(x_ref[...], vec_ref).astype(BF16)
    tm = h.shape[0]
    pt = lax.dot_general(wqv_ref[...], h, (((1,), (1,)), ((), ())), preferred_element_type=F32)
    pk = jnp.dot(h, wk_ref[...], preferred_element_type=F32)
    cost = cost_ref[...]
    sint = sint_ref[...]
    gq = gq_ref[...]
    qr = HEAD_DIM // 4
    for hq in range(Q_HEADS):
        t = pt[hq * HEAD_DIM:(hq + 1) * HEAD_DIM, :]
        t = t * lax.rsqrt(jnp.mean(t * t, axis=0, keepdims=True) + NORM_EPS) * gq
        rot = jnp.concatenate([-t[qr:2 * qr], t[0:qr], -t[3 * qr:4 * qr], t[2 * qr:3 * qr]], axis=0)
        qt = ((t * cost + rot * sint) * QK_SCALE_LOG2).astype(BF16)
        g = hq % GROUP
        for sub in range(tm // tq):
            q_ref[hq // GROUP, sub, :, g * tq:(g + 1) * tq] = qt[:, sub * tq:(sub + 1) * tq]
    qd = Q_HEADS * HEAD_DIM
    cos = cos_ref[...]
    sin = sin_ref[...]
    for hk in range(KV_HEADS):
        t = _head_rms(pk[:, hk * HEAD_DIM:(hk + 1) * HEAD_DIM], gk_ref[0:1, :])
        k_ref[hk] = _rope(t, cos, sin).astype(BF16)
        v_ref[hk, 0, 0:HEAD_DIM, :] = pt[qd + hk * HEAD_DIM:qd + (hk + 1) * HEAD_DIM, :].astype(BF16)
        v_ref[hk, 0, HEAD_DIM:, :] = jnp.ones((ATT_VROWS - HEAD_DIM, tm), BF16)


def _qkv_call(x, vec, wqv_t, wk, gq_b, gk8, cos, sin, cos_t, sin_t, tm, tq):
    t, d = x.shape
    nsub = tm // tq
    row = lambda i: (i, 0)
    col = lambda i: (0, i)
    return pl.pallas_call(
        functools.partial(_qkv_kernel, tq=tq),
        grid=(t // tm,),
        in_specs=[pl.BlockSpec((tm, d), row),
                  _full((8, d)),
                  _full(wqv_t.shape),
                  _full(wk.shape),
                  _full((HEAD_DIM, tm)),
                  _full((8, HEAD_DIM)),
                  pl.BlockSpec((tm, HEAD_DIM), row),
                  pl.BlockSpec((tm, HEAD_DIM), row),
                  pl.BlockSpec((HEAD_DIM, tm), col),
                  pl.BlockSpec((HEAD_DIM, tm), col)],
        out_specs=[pl.BlockSpec((KV_HEADS, nsub, HEAD_DIM, GROUP * tq), lambda i: (0, i, 0, 0)),
                   pl.BlockSpec((KV_HEADS, tm, HEAD_DIM), lambda i: (0, i, 0)),
                   pl.BlockSpec((KV_HEADS, 1, ATT_VROWS, tm), lambda i: (0, i, 0, 0))],
        out_shape=[jax.ShapeDtypeStruct((KV_HEADS, t // tq, HEAD_DIM, GROUP * tq), BF16),
                   jax.ShapeDtypeStruct((KV_HEADS, t, HEAD_DIM), BF16),
                   jax.ShapeDtypeStruct((KV_HEADS, t // tm, ATT_VROWS, tm), BF16)],
        compiler_params=_cparams("arbitrary"),
        name="gqa_qkv_proj",
    )(x, vec, wqv_t, wk, gq_b, gk8, cos, sin, cos_t, sin_t)


def _flash_kernel(*refs, tk, nk):
    if nk:
        q_ref, kc_ref, vc_ref, k_ref, v_ref, o_ref, m_ref, acc_ref, sa_ref, sb_ref = refs
    else:
        q_ref, kc_ref, vc_ref, o_ref, m_ref, acc_ref = refs
    qt = q_ref[0, 0]

    def scores(ks):
        return jnp.dot(ks, qt, preferred_element_type=F32)

    def absorb(st, vt):
        m_old = m_ref[...]
        m_new = jnp.maximum(m_old, jnp.max(st, axis=0, keepdims=True))
        alpha = jnp.exp2(m_old - m_new)
        pt = jnp.exp2(st - m_new).astype(BF16)
        acc_ref[...] = alpha * acc_ref[...] + jnp.dot(vt, pt, preferred_element_type=F32)
        m_ref[...] = m_new

    st = scores(kc_ref[0])
    m0 = jnp.max(st, axis=0, keepdims=True)
    acc_ref[...] = jnp.dot(vc_ref[0, 0], jnp.exp2(st - m0).astype(BF16), preferred_element_type=F32)
    m_ref[...] = m0

    if nk:
        def key_tile(j):
            return k_ref[0, pl.ds(pl.multiple_of(j * tk, tk), tk), :]

        sa_ref[...] = scores(key_tile(0))

        def body(jj, carry):
            j = 2 * jj
            sb_ref[...] = scores(key_tile(j + 1))
            absorb(sa_ref[...], v_ref[0, j])
            sa_ref[...] = scores(key_tile(j + 2))
            absorb(sb_ref[...], v_ref[0, j + 1])
            return carry

        lax.fori_loop(0, nk // 2 - 1, body, 0)
        sb_ref[...] = scores(key_tile(nk - 1))
        absorb(sa_ref[...], v_ref[0, nk - 2])
        absorb(sb_ref[...], v_ref[0, nk - 1])

    acc = acc_ref[...]
    out = acc[0:HEAD_DIM, :] / acc[HEAD_DIM:HEAD_DIM + 1, :]
    tq = o_ref.shape[0]
    for g in range(GROUP):
        o_ref[:, g * HEAD_DIM:(g + 1) * HEAD_DIM] = jnp.transpose(out[:, g * tq:(g + 1) * tq]).astype(o_ref.dtype)


def _flash_call(q, kc, vc, k, v):
    nq, cols = q.shape[1], q.shape[3]
    tq = cols // GROUP
    head3 = lambda h, i: (h, 0, 0)
    head4 = lambda h, i: (h, 0, 0, 0)
    in_specs = [pl.BlockSpec((1, 1, HEAD_DIM, cols), lambda h, i: (h, i, 0, 0)),
                pl.BlockSpec((1,) + kc.shape[1:], head3), pl.BlockSpec((1,) + vc.shape[1:], head4)]
    args = [q, kc, vc]
    scratch = [pltpu.VMEM((1, cols), F32), pltpu.VMEM((ATT_VROWS, cols), F32)]
    nk = tk = 0
    if k is not None:
        nk, tk = v.shape[1], v.shape[3]
        in_specs += [pl.BlockSpec((1,) + k.shape[1:], head3), pl.BlockSpec((1,) + v.shape[1:], head4)]
        args += [k, v]
        assert nk % 2 == 0 and nk >= 2
        scratch += [pltpu.VMEM((tk, cols), F32), pltpu.VMEM((tk, cols), F32)]
    return pl.pallas_call(
        functools.partial(_flash_kernel, tk=tk, nk=nk),
        grid=(KV_HEADS, nq),
        in_specs=in_specs,
        out_specs=pl.BlockSpec((tq, GROUP * HEAD_DIM), lambda h, i: (i, h)),
        out_shape=jax.ShapeDtypeStruct((nq * tq, Q_HEADS * HEAD_DIM), BF16),
        scratch_shapes=scratch,
        compiler_params=_cparams("arbitrary", "arbitrary"),
        name="gqa_flash",
    )(*args)


def _ml_proj_kernel(x_ref, vec_ref, wq_ref, wkt_ref, wv_ref, wg_ref, wgt_ref, bg_ref, bgt_ref,
                    q_ref, kt_ref, v_ref, gc_ref, gr_ref):
    h = _modulated(x_ref[...], vec_ref).astype(BF16)
    nt = (((1,), (1,)), ((), ()))
    q_ref[...] = jnp.dot(h, wq_ref[...], preferred_element_type=F32).astype(BF16)
    kt = lax.dot_general(wkt_ref[...], h, nt, preferred_element_type=F32)
    kt_ref[...] = (kt * (ML_DK ** -0.5)).astype(BF16)
    v_ref[...] = jnp.dot(h, wv_ref[...], preferred_element_type=F32).astype(BF16)
    gc_ref[...] = jnp.dot(h, wg_ref[...], preferred_element_type=F32) + bg_ref[...]
    gr_ref[...] = lax.dot_general(wgt_ref[...], h, nt, preferred_element_type=F32) + bgt_ref[...]


def _ml_proj_call(x, vec, wq, wkt, wv, wg, wgt, bg, bgt, tm):
    t, d = x.shape
    row = lambda i: (i, 0)
    col = lambda i: (0, i)
    return pl.pallas_call(
        _ml_proj_kernel,
        grid=(t // tm,),
        in_specs=[pl.BlockSpec((tm, d), row), _full((8, d)), _full(wq.shape), _full(wkt.shape), _full(wv.shape),
                  _full(wg.shape), _full(wgt.shape), _full(bg.shape), _full(bgt.shape)],
        out_specs=[pl.BlockSpec((tm, ML_KD), row), pl.BlockSpec((ML_KD, tm), col), pl.BlockSpec((tm, d), row),
                   pl.BlockSpec((tm, ML_GATES), row), pl.BlockSpec((ML_GATES, tm), col)],
        out_shape=[jax.ShapeDtypeStruct((t, ML_KD), BF16), jax.ShapeDtypeStruct((ML_KD, t), BF16),
                   jax.ShapeDtypeStruct((t, d), BF16), jax.ShapeDtypeStruct((t, ML_GATES), F32),
                   jax.ShapeDtypeStruct((ML_GATES, t), F32)],
        compiler_params=_cparams("arbitrary"),
        name="mlstm_in_proj",
    )(x, vec, wq, wkt, wv, wg, wgt, bg, bgt)


def _ml_scan_kernel(qf_ref, ktf_ref, vf_ref, gcf_ref, grf_ref, qb_ref, ktb_ref, vb_ref, gcb_ref, grb_ref,
                    c0_ref, m0_ref, hf_ref, hb_ref, cout_ref, mout_ref, c_ref, m_ref):
    n = pl.program_id(0)
    L = ML_CHUNK

    @pl.when(n == 0)
    def _():
        c_ref[...] = c0_ref[...]
        m_ref[...] = m0_ref[...]

    r = lax.broadcasted_iota(jnp.int32, (L, L), 0)
    s_ = lax.broadcasted_iota(jnp.int32, (L, L), 1)
    lower = r >= s_
    upper = r <= s_
    ones_blk = jnp.ones((L, ML_STATE_W - ML_DV), BF16)
    hi = lax.Precision.HIGHEST
    per_dir = ((qf_ref, ktf_ref, vf_ref, gcf_ref, grf_ref, hf_ref, lower, upper, L - 1),
               (qb_ref, ktb_ref, vb_ref, gcb_ref, grb_ref, hb_ref, upper, lower, 0))
    for d, (q_ref, kt_ref, v_ref, gc_ref, gr_ref, h_ref, seen, seen_t, end) in enumerate(per_dir):
        gcol = gc_ref[...]
        grow = gr_ref[...]
        a_col = jnp.dot(seen.astype(F32), _log_sigmoid(gcol), precision=hi, preferred_element_type=F32)
        a_row = jnp.dot(_log_sigmoid(grow), seen_t.astype(F32), precision=hi, preferred_element_type=F32)
        for hd in range(ML_HEADS):
            ji = (2 * d) * ML_HEADS + hd
            jf = (2 * d + 1) * ML_HEADS + hd
            st = d * ML_HEADS + hd
            a_c = a_col[:, jf:jf + 1]
            a_r = a_row[jf:jf + 1, :]
            b_r = grow[ji:ji + 1, :] - a_r
            a_end = a_r[:, end:end + 1]
            m_prev = m_ref[st:st + 1, 0:1]
            dm = jnp.where(seen, a_c + b_r, -jnp.inf)
            inter = a_c + m_prev
            m_t = jnp.maximum(jnp.max(dm, axis=1, keepdims=True), inter)
            w = jnp.exp(dm - m_t)
            qh = q_ref[:, hd * ML_DK:(hd + 1) * ML_DK]
            kth = kt_ref[hd * ML_DK:(hd + 1) * ML_DK, :]
            sw = jnp.dot(qh, kth, preferred_element_type=F32) * w
            sc = jnp.exp(inter - m_t)
            vp = jnp.concatenate([v_ref[:, hd * ML_DV:(hd + 1) * ML_DV], ones_blk], axis=1)
            cst = c_ref[st]
            hx = (jnp.dot(sw.astype(BF16), vp, preferred_element_type=F32)
                  + sc * jnp.dot(qh, cst.astype(BF16), preferred_element_type=F32))
            den = jnp.maximum(jnp.abs(hx[:, ML_DV:]), jnp.exp(-m_t))
            inv = 1.0 / den
            h_ref[:, hd * ML_DV:hd * ML_DV + 128] = hx[:, 0:128] * inv
            h_ref[:, hd * ML_DV + 128:(hd + 1) * ML_DV] = hx[:, 128:ML_DV] * inv
            w_end = a_end + b_r
            m_new = jnp.maximum(a_end + m_prev, jnp.max(w_end, axis=1, keepdims=True))
            e_r = jnp.exp(w_end - m_new)
            decay = jnp.exp(a_end + m_prev - m_new)
            ke = (kth.astype(F32) * e_r).astype(BF16)
            c_ref[st] = decay * cst + jnp.dot(ke, vp, preferred_element_type=F32)
            m_ref[st:st + 1, :] = jnp.broadcast_to(m_new, (1, 128))

    @pl.when(n == pl.num_programs(0) - 1)
    def _():
        cout_ref[...] = c_ref[...]
        mout_ref[...] = m_ref[...]


def _ml_scan_call(q, kt, v, gc, gr, c0, m0):
    t, d = v.shape
    L = ML_CHUNK
    nc = t // L
    fr = lambda n: (n, 0)
    fc = lambda n: (0, n)
    br = lambda n: (nc - 1 - n, 0)
    bc = lambda n: (0, nc - 1 - n)

    def specs(rm, cm):
        return [pl.BlockSpec((L, ML_KD), rm), pl.BlockSpec((ML_KD, L), cm), pl.BlockSpec((L, d), rm),
                pl.BlockSpec((L, ML_GATES), rm), pl.BlockSpec((ML_GATES, L), cm)]

    nst = 2 * ML_HEADS
    return pl.pallas_call(
        _ml_scan_kernel,
        grid=(nc,),
        in_specs=specs(fr, fc) + specs(br, bc) + [_full(c0.shape), _full(m0.shape)],
        out_specs=[pl.BlockSpec((L, d), fr), pl.BlockSpec((L, d), br), _full(c0.shape), _full(m0.shape)],
        out_shape=[jax.ShapeDtypeStruct((t, d), F32), jax.ShapeDtypeStruct((t, d), F32),
                   jax.ShapeDtypeStruct(c0.shape, F32), jax.ShapeDtypeStruct(m0.shape, F32)],
        scratch_shapes=[pltpu.VMEM((nst, ML_DK, ML_STATE_W), F32), pltpu.VMEM((nst, 128), F32)],
        compiler_params=_cparams("arbitrary"),
        name="mlstm_scan",
    )(q, kt, v, gc, gr, q, kt, v, gc, gr, c0, m0)


def _ml_out_kernel(x_ref, hf_ref, hb_ref, vec_ref, ng_ref, woi_ref, wo_ref, o_ref):
    x = x_ref[...]
    h = _modulated(x, vec_ref).astype(BF16)
    o = jnp.dot(h, woi_ref[...], preferred_element_type=F32)
    hs = hf_ref[...] + hb_ref[...]
    parts = []
    for hd in range(ML_HEADS):
        seg = hs[:, hd * ML_DV:(hd + 1) * ML_DV]
        parts.append(_head_rms(seg, ng_ref[0:1, hd * ML_DV:(hd + 1) * ML_DV]))
    hn = jnp.concatenate(parts, axis=1)
    y = jnp.dot((_sigmoid(o) * hn).astype(BF16), wo_ref[...], preferred_element_type=F32)
    o_ref[...] = x + vec_ref[2:3, :] * y


def _ml_out_call(x, hf, hb, vec, ng8, woi, wo, tm):
    t, d = x.shape
    row = lambda i: (i, 0)
    return pl.pallas_call(
        _ml_out_kernel,
        grid=(t // tm,),
        in_specs=[pl.BlockSpec((tm, d), row), pl.BlockSpec((tm, d), row), pl.BlockSpec((tm, d), row),
                  _full((8, d)), _full((8, d)), _full(woi.shape), _full(wo.shape)],
        out_specs=pl.BlockSpec((tm, d), row),
        out_shape=jax.ShapeDtypeStruct((t, d), F32),
        compiler_params=_cparams("arbitrary"),
        name="mlstm_out",
    )(x, hf, hb, vec, ng8, woi, wo)


def _pad8(rows):
    a = jnp.stack(rows)
    return jnp.concatenate([a, jnp.zeros((8 - a.shape[0], a.shape[1]), a.dtype)], axis=0)


def _rope_tables(n_tok):
    rows = n_tok // GRID_W
    seg = HEAD_DIM // 2
    inv = ROPE_THETA ** (-jnp.arange(seg // 2, dtype=F32) / (seg // 2))
    ang_r = jnp.arange(rows).astype(F32)[:, None] * inv
    ang_c = jnp.arange(GRID_W).astype(F32)[:, None] * inv

    def expand(fr, fc):
        r = jnp.repeat(fr, GRID_W, axis=0)
        c = jnp.tile(fc, (rows, 1))
        return jnp.concatenate([r, r, c, c], axis=-1)

    return expand(jnp.cos(ang_r), jnp.cos(ang_c)), expand(jnp.sin(ang_r), jnp.sin(ang_c))


def _ctx_read_at_or_after(i):
    return any((j % N_MIXERS) != 0 for j in range(i, DEPTH))


def kernel(x, c, ctx, c_ctx, mod_w, mod_b, norm_g, ffn_w13, ffn_w2, conv_w_in, conv_k, conv_w_out,
           attn_w_qkv, attn_q_g, attn_k_g, attn_w_o, mlstm_w_in, mlstm_b_gate, mlstm_norm_g, mlstm_w_o):
    d = D_MODEL
    t_lat = x.shape[1]
    t_ctx = ctx.shape[1]
    tm = 512
    tmc = t_ctx
    lat = x[0]
    cx = ctx[0]

    cvec = _pad8([c[0], c_ctx])
    modall = _mod_call(cvec, mod_w, mod_b)

    def vec(i, a, stream):
        m = modall[i, stream]
        base = 3 * a * d
        return _pad8([m[base:base + d], m[base + d:base + 2 * d], m[base + 2 * d:base + 3 * d], norm_g[i, a]])

    cos, sin = _rope_tables(t_lat)
    counters = [0, 0, 0]
    for i in range(DEPTH):
        kind = i % N_MIXERS
        j = counters[kind]
        counters[kind] += 1
        ctx_out = _ctx_read_at_or_after(i + 1)
        if not _ctx_read_at_or_after(i):
            cx = None
        w13a, w2a = ffn_w13[i, 0].astype(BF16), ffn_w2[i, 0].astype(BF16)
        w13b, w2b = ffn_w13[i, 1].astype(BF16), ffn_w2[i, 1].astype(BF16)
        lat = _ffn_call(lat, vec(i, 0, 0), w13a, w2a, tm)
        if cx is not None:
            cx = _ffn_call(cx, vec(i, 0, 1), w13a, w2a, tmc)
        vl, vc = vec(i, 1, 0), vec(i, 1, 1)
        if kind == 0:
            w_in, w_out = conv_w_in[j].astype(BF16), conv_w_out[j].astype(BF16)
            k8 = _pad8([conv_k[j, 0], conv_k[j, 1], conv_k[j, 2]])
            lat = _conv_call(lat, vl, w_in, k8, w_out, tm)
            if ctx_out:
                cx = _conv_call(cx, vc, w_in, k8, w_out, tmc)
        elif kind == 1:
            w_qkv, w_o = attn_w_qkv[j], attn_w_o[j].astype(BF16)
            qd, kd = Q_HEADS * HEAD_DIM, KV_HEADS * HEAD_DIM
            wqv_t = jnp.concatenate([w_qkv[:, :qd], w_qkv[:, qd + kd:]], axis=1).T.astype(BF16)
            wk = w_qkv[:, qd:qd + kd].astype(BF16)
            gk8 = _pad8([attn_k_g[j]])
            gq = attn_q_g[j][:, None]
            one, zero = jnp.ones((t_ctx, HEAD_DIM), F32), jnp.zeros((t_ctx, HEAD_DIM), F32)
            q, k, v = _qkv_call(lat, vl, wqv_t, wk, jnp.broadcast_to(gq, (HEAD_DIM, tm)), gk8,
                                cos, sin, cos.T, sin.T, tm, ATT_TQ)
            qc, kc, vc_ = _qkv_call(cx, vc, wqv_t, wk, jnp.broadcast_to(gq, (HEAD_DIM, tmc)), gk8,
                                    one, zero, one.T, zero.T, tmc, t_ctx)
            att = _flash_call(q, kc, vc_, k, v)
            lat = _proj_res_call(lat, att, vl, w_o, tm)
            if ctx_out:
                attc = _flash_call(qc, kc, vc_, None, None)
                cx = _proj_res_call(cx, attc, vc, w_o, tmc)
        else:
            w_in = mlstm_w_in[j]
            kd = ML_KD
            g0 = kd + d
            q0 = g0 + ML_GATES
            o0 = q0 + kd
            wkt = w_in[:, :kd].T.astype(BF16)
            wv = w_in[:, kd:g0].astype(BF16)
            wg = w_in[:, g0:q0].astype(BF16)
            wq = w_in[:, q0:o0].astype(BF16)
            woi = w_in[:, o0:].astype(BF16)
            bg = mlstm_b_gate[j].reshape(1, ML_GATES)
            pc = _ml_proj_call(cx, vc, wq, wkt, wv, wg, wg.T, bg, bg.T, tmc)
            pl_ = _ml_proj_call(lat, vl, wq, wkt, wv, wg, wg.T, bg, bg.T, tm)
            c0 = jnp.zeros((2 * ML_HEADS, ML_DK, ML_STATE_W), F32)
            m0 = jnp.zeros((2 * ML_HEADS, 128), F32)
            hfc, hbc, c1, m1 = _ml_scan_call(*pc, c0, m0)
            hf, hb, _, _ = _ml_scan_call(*pl_, c1, m1)
            ng8 = _pad8([mlstm_norm_g[j]])
            wo = mlstm_w_o[j].astype(BF16)
            lat = _ml_out_call(lat, hf, hb, vl, ng8, woi, wo, tm)
            if ctx_out:
                cx = _ml_out_call(cx, hfc, hbc, vc, ng8, woi, wo, tmc)
        if ctx_out:
            cx = _ffn_call(cx, vec(i, 2, 1), w13b, w2b, tmc)
        else:
            cx = None
        lat = _ffn_call(lat, vec(i, 2, 0), w13b, w2b, tm)
    return lat[None]
```

```python
import functools

import jax
import jax.numpy as jnp
from jax import lax
from jax.experimental import pallas as pl
from jax.experimental.pallas import tpu as pltpu

F32 = jnp.float32
BF16 = jnp.bfloat16

D_MODEL = 1024
DEPTH = 4
N_MIXERS = 3
NORM_EPS = 1e-6
FFN_HIDDEN = 2816
FFN_CHUNK = 256
GRID_W = 64
HEAD_DIM = 128
Q_HEADS = 8
KV_HEADS = 2
GROUP = Q_HEADS // KV_HEADS
ROPE_THETA = 10000.0
ML_HEADS = 4
ML_DV = 256
ML_DK = 128
ML_KD = ML_HEADS * ML_DK
ML_GATES = 4 * ML_HEADS
ML_CHUNK = 256
ML_STATE_W = ML_DV + 128
QK_SCALE_LOG2 = (HEAD_DIM ** -0.5) * 1.4426950408889634

VMEM_LIMIT = 56 * 1024 * 1024


def _cparams(*sem):
    return pltpu.CompilerParams(dimension_semantics=sem, vmem_limit_bytes=VMEM_LIMIT)


def _full(shape):
    return pl.BlockSpec(shape, lambda *_: (0,) * len(shape))


def _modulated(x, vec_ref):
    ms = jnp.mean(x * x, axis=-1, keepdims=True)
    y = x * lax.rsqrt(ms + NORM_EPS) * vec_ref[3:4, :]
    return y * (1.0 + vec_ref[1:2, :]) + vec_ref[0:1, :]


def _sigmoid(x):
    return 1.0 / (1.0 + jnp.exp(-x))


def _log_sigmoid(x):
    return jnp.minimum(x, 0.0) - jnp.log(1.0 + jnp.exp(-jnp.abs(x)))


def _mod_kernel(c_ref, w_ref, b_ref, o_ref):
    c = c_ref[...]
    s = (c * _sigmoid(c)).astype(BF16)
    o_ref[0] = jnp.dot(s, w_ref[0].astype(BF16), preferred_element_type=F32) + b_ref[0]


def _mod_call(cvec, mod_w, mod_b):
    depth, d, n = mod_w.shape
    tn = 1024
    return pl.pallas_call(
        _mod_kernel,
        grid=(depth, n // tn),
        in_specs=[_full((8, d)),
                  pl.BlockSpec((1, d, tn), lambda l, j: (l, 0, j)),
                  pl.BlockSpec((1, 1, tn), lambda l, j: (l, 0, j))],
        out_specs=pl.BlockSpec((1, 8, tn), lambda l, j: (l, 0, j)),
        out_shape=jax.ShapeDtypeStruct((depth, 8, n), F32),
        compiler_params=_cparams("arbitrary", "arbitrary"),
        name="adaln_mod",
    )(cvec, mod_w, mod_b.reshape(depth, 1, n))


def _ffn_kernel(x_ref, vec_ref, w13_ref, w2_ref, o_ref, acc_ref):
    x = x_ref[...]
    h = _modulated(x, vec_ref).astype(BF16)
    for j in range(FFN_HIDDEN // FFN_CHUNK):
        lo = j * FFN_CHUNK
        a = jnp.dot(h, w13_ref[:, lo:lo + FFN_CHUNK], preferred_element_type=F32)
        b = jnp.dot(h, w13_ref[:, FFN_HIDDEN + lo:FFN_HIDDEN + lo + FFN_CHUNK], preferred_element_type=F32)
        u = (a * _sigmoid(a) * b).astype(BF16)
        y = jnp.dot(u, w2_ref[lo:lo + FFN_CHUNK, :], preferred_element_type=F32)
        if j == 0:
            acc_ref[...] = y
        else:
            acc_ref[...] += y
    o_ref[...] = x + (0.5 * vec_ref[2:3, :]) * acc_ref[...]


def _ffn_call(x, vec, w13, w2, tm):
    t, d = x.shape
    return pl.pallas_call(
        _ffn_kernel,
        grid=(t // tm,),
        in_specs=[pl.BlockSpec((tm, d), lambda i: (i, 0)),
                  _full((8, d)),
                  _full(w13.shape),
                  _full(w2.shape)],
        out_specs=pl.BlockSpec((tm, d), lambda i: (i, 0)),
        out_shape=jax.ShapeDtypeStruct((t, d), F32),
        scratch_shapes=[pltpu.VMEM((tm, d), F32)],
        compiler_params=_cparams("arbitrary"),
        name="swiglu_half_step",
    )(x, vec, w13, w2)


def _proj_res_kernel(x_ref, a_ref, vec_ref, w_ref, o_ref):
    y = jnp.dot(a_ref[...], w_ref[...], preferred_element_type=F32)
    o_ref[...] = x_ref[...] + vec_ref[2:3, :] * y


def _proj_res_call(x, a, vec, w, tm):
    t, d = x.shape
    return pl.pallas_call(
        _proj_res_kernel,
        grid=(t // tm,),
        in_specs=[pl.BlockSpec((tm, d), lambda i: (i, 0)),
                  pl.BlockSpec((tm, a.shape[1]), lambda i: (i, 0)),
                  _full((8, d)),
                  _full(w.shape)],
        out_specs=pl.BlockSpec((tm, d), lambda i: (i, 0)),
        out_shape=jax.ShapeDtypeStruct((t, d), F32),
        compiler_params=_cparams("arbitrary"),
        name="out_proj_residual",
    )(x, a, vec, w)


CONV_HALO = 16


def _conv_kernel(x_ref, xp_ref, xn_ref, vec_ref, win_ref, k_ref, wout_ref, o_ref, h_ref, acc_ref, *, tm):
    i = pl.program_id(0)
    last = pl.num_programs(0) - 1
    x = x_ref[...]
    d = x.shape[1]
    keep_prev = (i > 0).astype(F32)
    keep_next = (i < last).astype(F32)
    h_ref[0:CONV_HALO, :] = (_modulated(xp_ref[...], vec_ref) * keep_prev).astype(BF16)
    h_ref[CONV_HALO:CONV_HALO + tm, :] = _modulated(x, vec_ref).astype(BF16)
    h_ref[CONV_HALO + tm:, :] = (_modulated(xn_ref[...], vec_ref) * keep_next).astype(BF16)
    rows = tm + 2 * CONV_HALO
    cw = 256
    for j in range(d // cw):
        lo = j * cw
        hall = h_ref[...]
        cg = jnp.dot(hall, win_ref[:, d + lo:d + lo + cw], preferred_element_type=F32)
        xv = jnp.dot(hall, win_ref[:, 2 * d + lo:2 * d + lo + cw], preferred_element_type=F32)
        u = cg * xv
        up = pltpu.roll(u, 1, axis=0)
        un = pltpu.roll(u, rows - 1, axis=0)
        kk = k_ref[:, lo:lo + cw]
        conv = kk[0:1, :] * up + kk[1:2, :] * u + kk[2:3, :] * un
        conv = conv[CONV_HALO:CONV_HALO + tm, :]
        bg = jnp.dot(h_ref[CONV_HALO:CONV_HALO + tm, :], win_ref[:, lo:lo + cw], preferred_element_type=F32)
        y = jnp.dot((bg * conv).astype(BF16), wout_ref[lo:lo + cw, :], preferred_element_type=F32)
        if j == 0:
            acc_ref[...] = y
        else:
            acc_ref[...] += y
    o_ref[...] = x + vec_ref[2:3, :] * acc_ref[...]


def _conv_call(x, vec, w_in, k8, w_out, tm):
    t, d = x.shape
    hb = tm // CONV_HALO
    nhb = t // CONV_HALO
    return pl.pallas_call(
        functools.partial(_conv_kernel, tm=tm),
        grid=(t // tm,),
        in_specs=[pl.BlockSpec((tm, d), lambda i: (i, 0)),
                  pl.BlockSpec((CONV_HALO, d), lambda i: (jnp.maximum(i * hb - 1, 0), 0)),
                  pl.BlockSpec((CONV_HALO, d), lambda i: (jnp.minimum((i + 1) * hb, nhb - 1), 0)),
                  _full((8, d)),
                  _full(w_in.shape),
                  _full((8, d)),
                  _full(w_out.shape)],
        out_specs=pl.BlockSpec((tm, d), lambda i: (i, 0)),
        out_shape=jax.ShapeDtypeStruct((t, d), F32),
        scratch_shapes=[pltpu.VMEM((tm + 2 * CONV_HALO, d), BF16), pltpu.VMEM((tm, d), F32)],
        compiler_params=_cparams("arbitrary"),
        name="short_conv_mixer",
    )(x, x, x, vec, w_in, k8, w_out)


def _head_rms(t, g):
    return t * lax.rsqrt(jnp.mean(t * t, axis=-1, keepdims=True) + NORM_EPS) * g


def _rope(t, cos, sin):
    lane = lax.broadcasted_iota(jnp.int32, t.shape, 1)
    first = (lane % (HEAD_DIM // 2)) < (HEAD_DIM // 4)
    rot = jnp.where(first, -pltpu.roll(t, HEAD_DIM - HEAD_DIM // 4, axis=1), pltpu.roll(t, HEAD_DIM // 4, axis=1))
    return t * cos + rot * sin


ATT_TQ = 256
ATT_LAG_MAX_EXP = 64.0
ATT_VROWS = HEAD_DIM + 16


def _qkv_kernel(x_ref, vec_ref, wqv_ref, wk_ref, gq_ref, gk_ref, cos_ref, sin_ref, cost_ref, sint_ref,
                q_ref, k_ref, v_ref, *, tq):
    h = _modulated(x_ref[...], vec_ref).astype(BF16)
    tm = h.shape[0]
    pt = lax.dot_general(wqv_ref[...], h, (((1,), (1,)), ((), ())), preferred_element_type=F32)
    pk = jnp.dot(h, wk_ref[...], preferred_element_type=F32)
    cost = cost_ref[...]
    sint = sint_ref[...]
    gq = gq_ref[...]
    qr = HEAD_DIM // 4
    for hq in range(Q_HEADS):
        t = pt[hq * HEAD_DIM:(hq + 1) * HEAD_DIM, :]
        t = t * lax.rsqrt(jnp.mean(t * t, axis=0, keepdims=True) + NORM_EPS) * gq
        rot = jnp.concatenate([-t[qr:2 * qr], t[0:qr], -t[3 * qr:4 * qr], t[2 * qr:3 * qr]], axis=0)
        qt = ((t * cost + rot * sint) * QK_SCALE_LOG2).astype(BF16)
        g = hq % GROUP
        for sub in range(tm // tq):
            q_ref[hq // GROUP, sub, :, g * tq:(g + 1) * tq] = qt[:, sub * tq:(sub + 1) * tq]
    qd = Q_HEADS * HEAD_DIM
    cos = cos_ref[...]
    sin = sin_ref[...]
    for hk in range(KV_HEADS):
        t = _head_rms(pk[:, hk * HEAD_DIM:(hk + 1) * HEAD_DIM], gk_ref[0:1, :])
        k_ref[hk] = _rope(t, cos, sin).astype(BF16)
        v_ref[hk, 0, 0:HEAD_DIM, :] = pt[qd + hk * HEAD_DIM:qd + (hk + 1) * HEAD_DIM, :].astype(BF16)
        v_ref[hk, 0, HEAD_DIM:, :] = jnp.ones((ATT_VROWS - HEAD_DIM, tm), BF16)


def _qkv_call(x, vec, wqv_t, wk, gq_b, gk8, cos, sin, cos_t, sin_t, tm, tq):
    t, d = x.shape
    nsub = tm // tq
    row = lambda i: (i, 0)
    col = lambda i: (0, i)
    return pl.pallas_call(
        functools.partial(_qkv_kernel, tq=tq),
        grid=(t // tm,),
        in_specs=[pl.BlockSpec((tm, d), row),
                  _full((8, d)),
                  _full(wqv_t.shape),
                  _full(wk.shape),
                  _full((HEAD_DIM, tm)),
                  _full((8, HEAD_DIM)),
                  pl.BlockSpec((tm, HEAD_DIM), row),
                  pl.BlockSpec((tm, HEAD_DIM), row),
                  pl.BlockSpec((HEAD_DIM, tm), col),
                  pl.BlockSpec((HEAD_DIM, tm), col)],
        out_specs=[pl.BlockSpec((KV_HEADS, nsub, HEAD_DIM, GROUP * tq), lambda i: (0, i, 0, 0)),
                   pl.BlockSpec((KV_HEADS, tm, HEAD_DIM), lambda i: (0, i, 0)),
                   pl.BlockSpec((KV_HEADS, 1, ATT_VROWS, tm), lambda i: (0, i, 0, 0))],
        out_shape=[jax.ShapeDtypeStruct((KV_HEADS, t // tq, HEAD_DIM, GROUP * tq), BF16),
                   jax.ShapeDtypeStruct((KV_HEADS, t, HEAD_DIM), BF16),
                   jax.ShapeDtypeStruct((KV_HEADS, t // tm, ATT_VROWS, tm), BF16)],
        compiler_params=_cparams("arbitrary"),
        name="gqa_qkv_proj",
    )(x, vec, wqv_t, wk, gq_b, gk8, cos, sin, cos_t, sin_t)


def _flash_kernel(*refs, tk, nk, lagged):
    if nk and not lagged:
        q_ref, kc_ref, vc_ref, k_ref, v_ref, o_ref, m_ref, acc_ref, sa_ref, sb_ref, xa_ref, xb_ref = refs
    elif nk:
        q_ref, kc_ref, vc_ref, k_ref, v_ref, o_ref, m_ref, acc_ref = refs
    else:
        q_ref, kc_ref, vc_ref, o_ref, m_ref, acc_ref = refs
    qt = q_ref[0, 0]

    def scores(ks):
        return jnp.dot(ks, qt, preferred_element_type=F32)

    def key_tile(j):
        return k_ref[0, pl.ds(pl.multiple_of(j * tk, tk), tk), :]

    def produce(j, s_ref, x_ref):
        st = scores(key_tile(j))
        s_ref[...] = st
        x_ref[...] = jnp.max(st, axis=0, keepdims=True)

    def absorb(s_ref, x_ref, vt):
        m_old = m_ref[...]
        m_new = jnp.maximum(m_old, x_ref[...])
        alpha = jnp.exp2(m_old - m_new)
        pt = jnp.exp2(s_ref[...] - m_new).astype(BF16)
        acc_ref[...] = alpha * acc_ref[...] + jnp.dot(vt, pt, preferred_element_type=F32)
        m_ref[...] = m_new

    def absorb_lagged(j):
        st = scores(key_tile(j))
        m_lag = m_ref[...]
        pt = jnp.exp2(st - m_lag).astype(BF16)
        m_new = jnp.maximum(m_lag, jnp.max(st, axis=0, keepdims=True))
        acc = acc_ref[...] + jnp.dot(v_ref[0, j], pt, preferred_element_type=F32)
        acc_ref[...] = acc * jnp.exp2(m_lag - m_new)
        m_ref[...] = m_new

    if nk and not lagged:
        produce(0, sa_ref, xa_ref)

    st = scores(kc_ref[0])
    m0 = jnp.max(st, axis=0, keepdims=True)
    acc_ref[...] = jnp.dot(vc_ref[0, 0], jnp.exp2(st - m0).astype(BF16), preferred_element_type=F32)
    m_ref[...] = m0

    if nk and lagged:
        def body(jj, carry):
            absorb_lagged(2 * jj)
            absorb_lagged(2 * jj + 1)
            return carry

        lax.fori_loop(0, nk // 2, body, 0)
    elif nk:
        def body(jj, carry):
            j = 2 * jj
            produce(j + 1, sb_ref, xb_ref)
            absorb(sa_ref, xa_ref, v_ref[0, j])
            produce(j + 2, sa_ref, xa_ref)
            absorb(sb_ref, xb_ref, v_ref[0, j + 1])
            return carry

        lax.fori_loop(0, nk // 2 - 1, body, 0)
        produce(nk - 1, sb_ref, xb_ref)
        absorb(sa_ref, xa_ref, v_ref[0, nk - 2])
        absorb(sb_ref, xb_ref, v_ref[0, nk - 1])

    acc = acc_ref[...]
    out = acc[0:HEAD_DIM, :] / acc[HEAD_DIM:HEAD_DIM + 1, :]
    tq = o_ref.shape[0]
    for g in range(GROUP):
        o_ref[:, g * HEAD_DIM:(g + 1) * HEAD_DIM] = jnp.transpose(out[:, g * tq:(g + 1) * tq]).astype(o_ref.dtype)


def _flash_call(q, kc, vc, k, v, lagged=False):
    nq, cols = q.shape[1], q.shape[3]
    tq = cols // GROUP
    head3 = lambda h, i: (h, 0, 0)
    head4 = lambda h, i: (h, 0, 0, 0)
    in_specs = [pl.BlockSpec((1, 1, HEAD_DIM, cols), lambda h, i: (h, i, 0, 0)),
                pl.BlockSpec((1,) + kc.shape[1:], head3), pl.BlockSpec((1,) + vc.shape[1:], head4)]
    args = [q, kc, vc]
    scratch = [pltpu.VMEM((1, cols), F32), pltpu.VMEM((ATT_VROWS, cols), F32)]
    nk = tk = 0
    if k is not None:
        nk, tk = v.shape[1], v.shape[3]
        in_specs += [pl.BlockSpec((1,) + k.shape[1:], head3), pl.BlockSpec((1,) + v.shape[1:], head4)]
        args += [k, v]
        assert nk % 2 == 0 and nk >= 2
        if not lagged:
            scratch += [pltpu.VMEM((tk, cols), F32), pltpu.VMEM((tk, cols), F32),
                        pltpu.VMEM((1, cols), F32), pltpu.VMEM((1, cols), F32)]
    return pl.pallas_call(
        functools.partial(_flash_kernel, tk=tk, nk=nk, lagged=lagged),
        grid=(KV_HEADS, nq),
        in_specs=in_specs,
        out_specs=pl.BlockSpec((tq, GROUP * HEAD_DIM), lambda h, i: (i, h)),
        out_shape=jax.ShapeDtypeStruct((nq * tq, Q_HEADS * HEAD_DIM), BF16),
        scratch_shapes=scratch,
        compiler_params=_cparams("arbitrary", "arbitrary"),
        name="gqa_flash_lagged" if lagged else "gqa_flash",
    )(*args)


def _ml_proj_kernel(x_ref, vec_ref, wq_ref, wkt_ref, wv_ref, wg_ref, wgt_ref, bg_ref, bgt_ref,
                    q_ref, kt_ref, v_ref, gc_ref, gr_ref):
    h = _modulated(x_ref[...], vec_ref).astype(BF16)
    nt = (((1,), (1,)), ((), ()))
    q_ref[...] = jnp.dot(h, wq_ref[...], preferred_element_type=F32).astype(BF16)
    kt = lax.dot_general(wkt_ref[...], h, nt, preferred_element_type=F32)
    kt_ref[...] = (kt * (ML_DK ** -0.5)).astype(BF16)
    v_ref[...] = jnp.dot(h, wv_ref[...], preferred_element_type=F32).astype(BF16)
    gc_ref[...] = jnp.dot(h, wg_ref[...], preferred_element_type=F32) + bg_ref[...]
    gr_ref[...] = lax.dot_general(wgt_ref[...], h, nt, preferred_element_type=F32) + bgt_ref[...]


def _ml_proj_call(x, vec, wq, wkt, wv, wg, wgt, bg, bgt, tm):
    t, d = x.shape
    row = lambda i: (i, 0)
    col = lambda i: (0, i)
    return pl.pallas_call(
        _ml_proj_kernel,
        grid=(t // tm,),
        in_specs=[pl.BlockSpec((tm, d), row), _full((8, d)), _full(wq.shape), _full(wkt.shape), _full(wv.shape),
                  _full(wg.shape), _full(wgt.shape), _full(bg.shape), _full(bgt.shape)],
        out_specs=[pl.BlockSpec((tm, ML_KD), row), pl.BlockSpec((ML_KD, tm), col), pl.BlockSpec((tm, d), row),
                   pl.BlockSpec((tm, ML_GATES), row), pl.BlockSpec((ML_GATES, tm), col)],
        out_shape=[jax.ShapeDtypeStruct((t, ML_KD), BF16), jax.ShapeDtypeStruct((ML_KD, t), BF16),
                   jax.ShapeDtypeStruct((t, d), BF16), jax.ShapeDtypeStruct((t, ML_GATES), F32),
                   jax.ShapeDtypeStruct((ML_GATES, t), F32)],
        compiler_params=_cparams("arbitrary"),
        name="mlstm_in_proj",
    )(x, vec, wq, wkt, wv, wg, wgt, bg, bgt)


def _ml_scan_kernel(qf_ref, ktf_ref, vf_ref, gcf_ref, grf_ref, qb_ref, ktb_ref, vb_ref, gcb_ref, grb_ref,
                    c0_ref, m0_ref, hf_ref, hb_ref, cout_ref, mout_ref, c_ref, m_ref):
    n = pl.program_id(0)
    L = ML_CHUNK

    @pl.when(n == 0)
    def _():
        c_ref[...] = c0_ref[...]
        m_ref[...] = m0_ref[...]

    r = lax.broadcasted_iota(jnp.int32, (L, L), 0)
    s_ = lax.broadcasted_iota(jnp.int32, (L, L), 1)
    lower = r >= s_
    upper = r <= s_
    ones_blk = jnp.ones((L, ML_STATE_W - ML_DV), BF16)
    hi = lax.Precision.HIGHEST
    per_dir = ((qf_ref, ktf_ref, vf_ref, gcf_ref, grf_ref, hf_ref, lower, upper, L - 1),
               (qb_ref, ktb_ref, vb_ref, gcb_ref, grb_ref, hb_ref, upper, lower, 0))
    for d, (q_ref, kt_ref, v_ref, gc_ref, gr_ref, h_ref, seen, seen_t, end) in enumerate(per_dir):
        gcol = gc_ref[...]
        grow = gr_ref[...]
        a_col = jnp.dot(seen.astype(F32), _log_sigmoid(gcol), precision=hi, preferred_element_type=F32)
        a_row = jnp.dot(_log_sigmoid(grow), seen_t.astype(F32), precision=hi, preferred_element_type=F32)
        for hd in range(ML_HEADS):
            ji = (2 * d) * ML_HEADS + hd
            jf = (2 * d + 1) * ML_HEADS + hd
            st = d * ML_HEADS + hd
            a_c = a_col[:, jf:jf + 1]
            a_r = a_row[jf:jf + 1, :]
            b_r = grow[ji:ji + 1, :] - a_r
            a_end = a_r[:, end:end + 1]
            m_prev = m_ref[st:st + 1, 0:1]
            dm = jnp.where(seen, a_c + b_r, -jnp.inf)
            inter = a_c + m_prev
            m_t = jnp.maximum(jnp.max(dm, axis=1, keepdims=True), inter)
            w = jnp.exp(dm - m_t)
            qh = q_ref[:, hd * ML_DK:(hd + 1) * ML_DK]
            kth = kt_ref[hd * ML_DK:(hd + 1) * ML_DK, :]
            sw = jnp.dot(qh, kth, preferred_element_type=F32) * w
            sc = jnp.exp(inter - m_t)
            vp = jnp.concatenate([v_ref[:, hd * ML_DV:(hd + 1) * ML_DV], ones_blk], axis=1)
            cst = c_ref[st]
            hx = (jnp.dot(sw.astype(BF16), vp, preferred_element_type=F32)
                  + sc * jnp.dot(qh, cst.astype(BF16), preferred_element_type=F32))
            den = jnp.maximum(jnp.abs(hx[:, ML_DV:]), jnp.exp(-m_t))
            inv = 1.0 / den
            h_ref[:, hd * ML_DV:hd * ML_DV + 128] = hx[:, 0:128] * inv
            h_ref[:, hd * ML_DV + 128:(hd + 1) * ML_DV] = hx[:, 128:ML_DV] * inv
            w_end = a_end + b_r
            m_new = jnp.maximum(a_end + m_prev, jnp.max(w_end, axis=1, keepdims=True))
            e_r = jnp.exp(w_end - m_new)
            decay = jnp.exp(a_end + m_prev - m_new)
            ke = (kth.astype(F32) * e_r).astype(BF16)
            c_ref[st] = decay * cst + jnp.dot(ke, vp, preferred_element_type=F32)
            m_ref[st:st + 1, :] = jnp.broadcast_to(m_new, (1, 128))

    @pl.when(n == pl.num_programs(0) - 1)
    def _():
        cout_ref[...] = c_ref[...]
        mout_ref[...] = m_ref[...]


def _ml_scan_call(q, kt, v, gc, gr, c0, m0):
    t, d = v.shape
    L = ML_CHUNK
    nc = t // L
    fr = lambda n: (n, 0)
    fc = lambda n: (0, n)
    br = lambda n: (nc - 1 - n, 0)
    bc = lambda n: (0, nc - 1 - n)

    def specs(rm, cm):
        return [pl.BlockSpec((L, ML_KD), rm), pl.BlockSpec((ML_KD, L), cm), pl.BlockSpec((L, d), rm),
                pl.BlockSpec((L, ML_GATES), rm), pl.BlockSpec((ML_GATES, L), cm)]

    nst = 2 * ML_HEADS
    return pl.pallas_call(
        _ml_scan_kernel,
        grid=(nc,),
        in_specs=specs(fr, fc) + specs(br, bc) + [_full(c0.shape), _full(m0.shape)],
        out_specs=[pl.BlockSpec((L, d), fr), pl.BlockSpec((L, d), br), _full(c0.shape), _full(m0.shape)],
        out_shape=[jax.ShapeDtypeStruct((t, d), F32), jax.ShapeDtypeStruct((t, d), F32),
                   jax.ShapeDtypeStruct(c0.shape, F32), jax.ShapeDtypeStruct(m0.shape, F32)],
        scratch_shapes=[pltpu.VMEM((nst, ML_DK, ML_STATE_W), F32), pltpu.VMEM((nst, 128), F32)],
        compiler_params=_cparams("arbitrary"),
        name="mlstm_scan",
    )(q, kt, v, gc, gr, q, kt, v, gc, gr, c0, m0)


def _ml_out_kernel(x_ref, hf_ref, hb_ref, vec_ref, ng_ref, woi_ref, wo_ref, o_ref):
    x = x_ref[...]
    h = _modulated(x, vec_ref).astype(BF16)
    o = jnp.dot(h, woi_ref[...], preferred_element_type=F32)
    hs = hf_ref[...] + hb_ref[...]
    parts = []
    for hd in range(ML_HEADS):
        seg = hs[:, hd * ML_DV:(hd + 1) * ML_DV]
        parts.append(_head_rms(seg, ng_ref[0:1, hd * ML_DV:(hd + 1) * ML_DV]))
    hn = jnp.concatenate(parts, axis=1)
    y = jnp.dot((_sigmoid(o) * hn).astype(BF16), wo_ref[...], preferred_element_type=F32)
    o_ref[...] = x + vec_ref[2:3, :] * y


def _ml_out_call(x, hf, hb, vec, ng8, woi, wo, tm):
    t, d = x.shape
    row = lambda i: (i, 0)
    return pl.pallas_call(
        _ml_out_kernel,
        grid=(t // tm,),
        in_specs=[pl.BlockSpec((tm, d), row), pl.BlockSpec((tm, d), row), pl.BlockSpec((tm, d), row),
                  _full((8, d)), _full((8, d)), _full(woi.shape), _full(wo.shape)],
        out_specs=pl.BlockSpec((tm, d), row),
        out_shape=jax.ShapeDtypeStruct((t, d), F32),
        compiler_params=_cparams("arbitrary"),
        name="mlstm_out",
    )(x, hf, hb, vec, ng8, woi, wo)


def _pad8(rows):
    a = jnp.stack(rows)
    return jnp.concatenate([a, jnp.zeros((8 - a.shape[0], a.shape[1]), a.dtype)], axis=0)


def _rope_tables(n_tok):
    rows = n_tok // GRID_W
    seg = HEAD_DIM // 2
    inv = ROPE_THETA ** (-jnp.arange(seg // 2, dtype=F32) / (seg // 2))
    ang_r = jnp.arange(rows).astype(F32)[:, None] * inv
    ang_c = jnp.arange(GRID_W).astype(F32)[:, None] * inv

    def expand(fr, fc):
        r = jnp.repeat(fr, GRID_W, axis=0)
        c = jnp.tile(fc, (rows, 1))
        return jnp.concatenate([r, r, c, c], axis=-1)

    return expand(jnp.cos(ang_r), jnp.cos(ang_c)), expand(jnp.sin(ang_r), jnp.sin(ang_c))


def _ctx_read_at_or_after(i):
    return any((j % N_MIXERS) != 0 for j in range(i, DEPTH))


def kernel(x, c, ctx, c_ctx, mod_w, mod_b, norm_g, ffn_w13, ffn_w2, conv_w_in, conv_k, conv_w_out,
           attn_w_qkv, attn_q_g, attn_k_g, attn_w_o, mlstm_w_in, mlstm_b_gate, mlstm_norm_g, mlstm_w_o):
    d = D_MODEL
    t_lat = x.shape[1]
    t_ctx = ctx.shape[1]
    tm = 512
    tmc = t_ctx
    lat = x[0]
    cx = ctx[0]

    cvec = _pad8([c[0], c_ctx])
    modall = _mod_call(cvec, mod_w, mod_b)

    def vec(i, a, stream):
        m = modall[i, stream]
        base = 3 * a * d
        return _pad8([m[base:base + d], m[base + d:base + 2 * d], m[base + 2 * d:base + 3 * d], norm_g[i, a]])

    cos, sin = _rope_tables(t_lat)
    counters = [0, 0, 0]
    for i in range(DEPTH):
        kind = i % N_MIXERS
        j = counters[kind]
        counters[kind] += 1
        ctx_out = _ctx_read_at_or_after(i + 1)
        if not _ctx_read_at_or_after(i):
            cx = None
        w13a, w2a = ffn_w13[i, 0].astype(BF16), ffn_w2[i, 0].astype(BF16)
        w13b, w2b = ffn_w13[i, 1].astype(BF16), ffn_w2[i, 1].astype(BF16)
        lat = _ffn_call(lat, vec(i, 0, 0), w13a, w2a, tm)
        if cx is not None:
            cx = _ffn_call(cx, vec(i, 0, 1), w13a, w2a, tmc)
        vl, vc = vec(i, 1, 0), vec(i, 1, 1)
        if kind == 0:
            w_in, w_out = conv_w_in[j].astype(BF16), conv_w_out[j].astype(BF16)
            k8 = _pad8([conv_k[j, 0], conv_k[j, 1], conv_k[j, 2]])
            lat = _conv_call(lat, vl, w_in, k8, w_out, tm)
            if ctx_out:
                cx = _conv_call(cx, vc, w_in, k8, w_out, tmc)
        elif kind == 1:
            w_qkv, w_o = attn_w_qkv[j], attn_w_o[j].astype(BF16)
            qd, kd = Q_HEADS * HEAD_DIM, KV_HEADS * HEAD_DIM
            wqv_t = jnp.concatenate([w_qkv[:, :qd], w_qkv[:, qd + kd:]], axis=1).T.astype(BF16)
            wk = w_qkv[:, qd:qd + kd].astype(BF16)
            gk8 = _pad8([attn_k_g[j]])
            gq = attn_q_g[j][:, None]
            one, zero = jnp.ones((t_ctx, HEAD_DIM), F32), jnp.zeros((t_ctx, HEAD_DIM), F32)
            q, k, v = _qkv_call(lat, vl, wqv_t, wk, jnp.broadcast_to(gq, (HEAD_DIM, tm)), gk8,
                                cos, sin, cos.T, sin.T, tm, ATT_TQ)
            qc, kc, vc_ = _qkv_call(cx, vc, wqv_t, wk, jnp.broadcast_to(gq, (HEAD_DIM, tmc)), gk8,
                                    one, zero, one.T, zero.T, tmc, t_ctx)
            bound = (HEAD_DIM * QK_SCALE_LOG2) * jnp.max(jnp.abs(attn_q_g[j])) * jnp.max(jnp.abs(attn_k_g[j]))
            att = lax.cond(2.0 * bound <= ATT_LAG_MAX_EXP,
                           lambda *a: _flash_call(*a, lagged=True), lambda *a: _flash_call(*a), q, kc, vc_, k, v)
            lat = _proj_res_call(lat, att, vl, w_o, tm)
            if ctx_out:
                attc = _flash_call(qc, kc, vc_, None, None)
                cx = _proj_res_call(cx, attc, vc, w_o, tmc)
        else:
            w_in = mlstm_w_in[j]
            kd = ML_KD
            g0 = kd + d
            q0 = g0 + ML_GATES
            o0 = q0 + kd
            wkt = w_in[:, :kd].T.astype(BF16)
            wv = w_in[:, kd:g0].astype(BF16)
            wg = w_in[:, g0:q0].astype(BF16)
            wq = w_in[:, q0:o0].astype(BF16)
            woi = w_in[:, o0:].astype(BF16)
            bg = mlstm_b_gate[j].reshape(1, ML_GATES)
            pc = _ml_proj_call(cx, vc, wq, wkt, wv, wg, wg.T, bg, bg.T, tmc)
            pl_ = _ml_proj_call(lat, vl, wq, wkt, wv, wg, wg.T, bg, bg.T, tm)
            c0 = jnp.zeros((2 * ML_HEADS, ML_DK, ML_STATE_W), F32)
            m0 = jnp.zeros((2 * ML_HEADS, 128), F32)
            hfc, hbc, c1, m1 = _ml_scan_call(*pc, c0, m0)
            hf, hb, _, _ = _ml_scan_call(*pl_, c1, m1)
            ng8 = _pad8([mlstm_norm_g[j]])
            wo = mlstm_w_o[j].astype(BF16)
            lat = _ml_out_call(lat, hf, hb, vl, ng8, woi, wo, tm)
            if ctx_out:
                cx = _ml_out_call(cx, hfc, hbc, vc, ng8, woi, wo, tmc)
        if ctx_out:
            cx = _ffn_call(cx, vec(i, 2, 1), w13b, w2b, tmc)
        else:
            cx = None
        lat = _ffn_call(lat, vec(i, 2, 0), w13b, w2b, tm)
    return lat[None]
```

```python
import functools

import jax
import jax.numpy as jnp
from jax import lax
from jax.experimental import pallas as pl
from jax.experimental.pallas import tpu as pltpu

F32 = jnp.float32
BF16 = jnp.bfloat16

D_MODEL = 1024
DEPTH = 4
N_MIXERS = 3
NORM_EPS = 1e-6
FFN_HIDDEN = 2816
FFN_CHUNK = 256
GRID_W = 64
HEAD_DIM = 128
Q_HEADS = 8
KV_HEADS = 2
GROUP = Q_HEADS // KV_HEADS
ROPE_THETA = 10000.0
ML_HEADS = 4
ML_DV = 256
ML_DK = 128
ML_KD = ML_HEADS * ML_DK
ML_GATES = 4 * ML_HEADS
ML_CHUNK = 256
ML_STATE_W = ML_DV + 128
QK_SCALE_LOG2 = (HEAD_DIM ** -0.5) * 1.4426950408889634

VMEM_LIMIT = 56 * 1024 * 1024


def _cparams(*sem):
    return pltpu.CompilerParams(dimension_semantics=sem, vmem_limit_bytes=VMEM_LIMIT)


def _full(shape):
    return pl.BlockSpec(shape, lambda *_: (0,) * len(shape))


def _modulated(x, vec_ref):
    ms = jnp.mean(x * x, axis=-1, keepdims=True)
    y = x * lax.rsqrt(ms + NORM_EPS) * vec_ref[3:4, :]
    return y * (1.0 + vec_ref[1:2, :]) + vec_ref[0:1, :]


def _sigmoid(x):
    return 1.0 / (1.0 + jnp.exp(-x))


def _log_sigmoid(x):
    return jnp.minimum(x, 0.0) - jnp.log(1.0 + jnp.exp(-jnp.abs(x)))


def _mod_kernel(c_ref, w_ref, b_ref, o_ref):
    c = c_ref[...]
    s = (c * _sigmoid(c)).astype(BF16)
    o_ref[0] = jnp.dot(s, w_ref[0].astype(BF16), preferred_element_type=F32) + b_ref[0]


def _mod_call(cvec, mod_w, mod_b):
    depth, d, n = mod_w.shape
    tn = 1024
    return pl.pallas_call(
        _mod_kernel,
        grid=(depth, n // tn),
        in_specs=[_full((8, d)),
                  pl.BlockSpec((1, d, tn), lambda l, j: (l, 0, j)),
                  pl.BlockSpec((1, 1, tn), lambda l, j: (l, 0, j))],
        out_specs=pl.BlockSpec((1, 8, tn), lambda l, j: (l, 0, j)),
        out_shape=jax.ShapeDtypeStruct((depth, 8, n), F32),
        compiler_params=_cparams("arbitrary", "arbitrary"),
        name="adaln_mod",
    )(cvec, mod_w, mod_b.reshape(depth, 1, n))


def _ffn_kernel(x_ref, vec_ref, w13_ref, w2_ref, o_ref, acc_ref):
    x = x_ref[...]
    h = _modulated(x, vec_ref).astype(BF16)
    for j in range(FFN_HIDDEN // FFN_CHUNK):
        lo = j * FFN_CHUNK
        hi = FFN_HIDDEN + lo
        a = jnp.dot(h, w13_ref[:, lo:lo + FFN_CHUNK].astype(BF16), preferred_element_type=F32)
        b = jnp.dot(h, w13_ref[:, hi:hi + FFN_CHUNK].astype(BF16), preferred_element_type=F32)
        u = (a * _sigmoid(a) * b).astype(BF16)
        y = jnp.dot(u, w2_ref[lo:lo + FFN_CHUNK, :].astype(BF16), preferred_element_type=F32)
        if j == 0:
            acc_ref[...] = y
        else:
            acc_ref[...] += y
    o_ref[...] = x + (0.5 * vec_ref[2:3, :]) * acc_ref[...]


def _ffn_call(x, vec, w13_all, w2_all, layer, which, tm):
    t, d = x.shape
    pick = lambda i: (layer, which, 0, 0)
    return pl.pallas_call(
        _ffn_kernel,
        grid=(t // tm,),
        in_specs=[pl.BlockSpec((tm, d), lambda i: (i, 0)),
                  _full((8, d)),
                  pl.BlockSpec((None, None) + w13_all.shape[2:], pick, pipeline_mode=pl.Buffered(1)),
                  pl.BlockSpec((None, None) + w2_all.shape[2:], pick, pipeline_mode=pl.Buffered(1))],
        out_specs=pl.BlockSpec((tm, d), lambda i: (i, 0)),
        out_shape=jax.ShapeDtypeStruct((t, d), F32),
        scratch_shapes=[pltpu.VMEM((tm, d), F32)],
        compiler_params=_cparams("arbitrary"),
        name="swiglu_half_step",
    )(x, vec, w13_all, w2_all)


def _proj_res_kernel(x_ref, a_ref, vec_ref, w_ref, o_ref):
    y = jnp.dot(a_ref[...], w_ref[...], preferred_element_type=F32)
    o_ref[...] = x_ref[...] + vec_ref[2:3, :] * y


def _proj_res_call(x, a, vec, w, tm):
    t, d = x.shape
    return pl.pallas_call(
        _proj_res_kernel,
        grid=(t // tm,),
        in_specs=[pl.BlockSpec((tm, d), lambda i: (i, 0)),
                  pl.BlockSpec((tm, a.shape[1]), lambda i: (i, 0)),
                  _full((8, d)),
                  _full(w.shape)],
        out_specs=pl.BlockSpec((tm, d), lambda i: (i, 0)),
        out_shape=jax.ShapeDtypeStruct((t, d), F32),
        compiler_params=_cparams("arbitrary"),
        name="out_proj_residual",
    )(x, a, vec, w)


CONV_HALO = 16
CONV_TM = 1024


def _conv_kernel(x_ref, xp_ref, xn_ref, vec_ref, win_ref, k_ref, wout_ref, o_ref, h_ref, acc_ref, *, tm):
    i = pl.program_id(0)
    last = pl.num_programs(0) - 1
    x = x_ref[...]
    d = x.shape[1]
    keep_prev = (i > 0).astype(F32)
    keep_next = (i < last).astype(F32)
    h_ref[0:CONV_HALO, :] = (_modulated(xp_ref[...], vec_ref) * keep_prev).astype(BF16)
    h_ref[CONV_HALO:CONV_HALO + tm, :] = _modulated(x, vec_ref).astype(BF16)
    h_ref[CONV_HALO + tm:, :] = (_modulated(xn_ref[...], vec_ref) * keep_next).astype(BF16)
    rows = tm + 2 * CONV_HALO
    cw = 256
    for j in range(d // cw):
        lo = j * cw
        hall = h_ref[...]
        cg = jnp.dot(hall, win_ref[:, d + lo:d + lo + cw], preferred_element_type=F32)
        xv = jnp.dot(hall, win_ref[:, 2 * d + lo:2 * d + lo + cw], preferred_element_type=F32)
        u = cg * xv
        up = pltpu.roll(u, 1, axis=0)
        un = pltpu.roll(u, rows - 1, axis=0)
        kk = k_ref[:, lo:lo + cw]
        conv = kk[0:1, :] * up + kk[1:2, :] * u + kk[2:3, :] * un
        conv = conv[CONV_HALO:CONV_HALO + tm, :]
        bg = jnp.dot(h_ref[CONV_HALO:CONV_HALO + tm, :], win_ref[:, lo:lo + cw], preferred_element_type=F32)
        y = jnp.dot((bg * conv).astype(BF16), wout_ref[lo:lo + cw, :], preferred_element_type=F32)
        if j == 0:
            acc_ref[...] = y
        else:
            acc_ref[...] += y
    o_ref[...] = x + vec_ref[2:3, :] * acc_ref[...]


def _conv_call(x, vec, w_in_all, k8, w_out_all, j, tm):
    t, d = x.shape
    hb = tm // CONV_HALO
    nhb = t // CONV_HALO
    pick = lambda i: (j, 0, 0)
    return pl.pallas_call(
        functools.partial(_conv_kernel, tm=tm),
        grid=(t // tm,),
        in_specs=[pl.BlockSpec((tm, d), lambda i: (i, 0)),
                  pl.BlockSpec((CONV_HALO, d), lambda i: (jnp.maximum(i * hb - 1, 0), 0)),
                  pl.BlockSpec((CONV_HALO, d), lambda i: (jnp.minimum((i + 1) * hb, nhb - 1), 0)),
                  _full((8, d)),
                  pl.BlockSpec((None,) + w_in_all.shape[1:], pick),
                  _full((8, d)),
                  pl.BlockSpec((None,) + w_out_all.shape[1:], pick)],
        out_specs=pl.BlockSpec((tm, d), lambda i: (i, 0)),
        out_shape=jax.ShapeDtypeStruct((t, d), F32),
        scratch_shapes=[pltpu.VMEM((tm + 2 * CONV_HALO, d), BF16), pltpu.VMEM((tm, d), F32)],
        compiler_params=_cparams("arbitrary"),
        name="short_conv_mixer",
    )(x, x, x, vec, w_in_all, k8, w_out_all)


def _head_rms(t, g):
    return t * lax.rsqrt(jnp.mean(t * t, axis=-1, keepdims=True) + NORM_EPS) * g


def _rope(t, cos, sin):
    lane = lax.broadcasted_iota(jnp.int32, t.shape, 1)
    first = (lane % (HEAD_DIM // 2)) < (HEAD_DIM // 4)
    rot = jnp.where(first, -pltpu.roll(t, HEAD_DIM - HEAD_DIM // 4, axis=1), pltpu.roll(t, HEAD_DIM // 4, axis=1))
    return t * cos + rot * sin


ATT_TQ = 256
ATT_LAG_MAX_EXP = 64.0
ATT_VROWS = HEAD_DIM + 16


def _qkv_kernel(x_ref, vec_ref, wqv_ref, wk_ref, gq_ref, gk_ref, cos_ref, sin_ref, cost_ref, sint_ref,
                q_ref, k_ref, v_ref, *, tq):
    h = _modulated(x_ref[...], vec_ref).astype(BF16)
    tm = h.shape[0]
    pt = lax.dot_general(wqv_ref[...], h, (((1,), (1,)), ((), ())), preferred_element_type=F32)
    pk = jnp.dot(h, wk_ref[...], preferred_element_type=F32)
    cost = cost_ref[...]
    sint = sint_ref[...]
    gq = gq_ref[...]
    qr = HEAD_DIM // 4
    for hq in range(Q_HEADS):
        t = pt[hq * HEAD_DIM:(hq + 1) * HEAD_DIM, :]
        t = t * lax.rsqrt(jnp.mean(t * t, axis=0, keepdims=True) + NORM_EPS) * gq
        rot = jnp.concatenate([-t[qr:2 * qr], t[0:qr], -t[3 * qr:4 * qr], t[2 * qr:3 * qr]], axis=0)
        qt = ((t * cost + rot * sint) * QK_SCALE_LOG2).astype(BF16)
        g = hq % GROUP
        for sub in range(tm // tq):
            q_ref[hq // GROUP, sub, :, g * tq:(g + 1) * tq] = qt[:, sub * tq:(sub + 1) * tq]
    qd = Q_HEADS * HEAD_DIM
    cos = cos_ref[...]
    sin = sin_ref[...]
    for hk in range(KV_HEADS):
        t = _head_rms(pk[:, hk * HEAD_DIM:(hk + 1) * HEAD_DIM], gk_ref[0:1, :])
        k_ref[hk] = _rope(t, cos, sin).astype(BF16)
        v_ref[hk, 0, 0:HEAD_DIM, :] = pt[qd + hk * HEAD_DIM:qd + (hk + 1) * HEAD_DIM, :].astype(BF16)
        v_ref[hk, 0, HEAD_DIM:, :] = jnp.ones((ATT_VROWS - HEAD_DIM, tm), BF16)


def _qkv_call(x, vec, wqv_t, wk, gq_b, gk8, cos, sin, cos_t, sin_t, tm, tq):
    t, d = x.shape
    nsub = tm // tq
    row = lambda i: (i, 0)
    col = lambda i: (0, i)
    return pl.pallas_call(
        functools.partial(_qkv_kernel, tq=tq),
        grid=(t // tm,),
        in_specs=[pl.BlockSpec((tm, d), row),
                  _full((8, d)),
                  _full(wqv_t.shape),
                  _full(wk.shape),
                  _full((HEAD_DIM, tm)),
                  _full((8, HEAD_DIM)),
                  pl.BlockSpec((tm, HEAD_DIM), row),
                  pl.BlockSpec((tm, HEAD_DIM), row),
                  pl.BlockSpec((HEAD_DIM, tm), col),
                  pl.BlockSpec((HEAD_DIM, tm), col)],
        out_specs=[pl.BlockSpec((KV_HEADS, nsub, HEAD_DIM, GROUP * tq), lambda i: (0, i, 0, 0)),
                   pl.BlockSpec((KV_HEADS, tm, HEAD_DIM), lambda i: (0, i, 0)),
                   pl.BlockSpec((KV_HEADS, 1, ATT_VROWS, tm), lambda i: (0, i, 0, 0))],
        out_shape=[jax.ShapeDtypeStruct((KV_HEADS, t // tq, HEAD_DIM, GROUP * tq), BF16),
                   jax.ShapeDtypeStruct((KV_HEADS, t, HEAD_DIM), BF16),
                   jax.ShapeDtypeStruct((KV_HEADS, t // tm, ATT_VROWS, tm), BF16)],
        compiler_params=_cparams("arbitrary"),
        name="gqa_qkv_proj",
    )(x, vec, wqv_t, wk, gq_b, gk8, cos, sin, cos_t, sin_t)


def _flash_kernel(*refs, tk, nk, lagged):
    if nk and not lagged:
        q_ref, kc_ref, vc_ref, k_ref, v_ref, o_ref, m_ref, acc_ref, sa_ref, sb_ref, xa_ref, xb_ref = refs
    elif nk:
        q_ref, kc_ref, vc_ref, k_ref, v_ref, o_ref, m_ref, acc_ref = refs
    else:
        q_ref, kc_ref, vc_ref, o_ref, m_ref, acc_ref = refs
    qt = q_ref[0, 0]

    def scores(ks):
        return jnp.dot(ks, qt, preferred_element_type=F32)

    def key_tile(j):
        return k_ref[0, pl.ds(pl.multiple_of(j * tk, tk), tk), :]

    def produce(j, s_ref, x_ref):
        st = scores(key_tile(j))
        s_ref[...] = st
        x_ref[...] = jnp.max(st, axis=0, keepdims=True)

    def absorb(s_ref, x_ref, vt):
        m_old = m_ref[...]
        m_new = jnp.maximum(m_old, x_ref[...])
        alpha = jnp.exp2(m_old - m_new)
        pt = jnp.exp2(s_ref[...] - m_new).astype(BF16)
        acc_ref[...] = alpha * acc_ref[...] + jnp.dot(vt, pt, preferred_element_type=F32)
        m_ref[...] = m_new

    def absorb_lagged(j):
        st = scores(key_tile(j))
        m_lag = m_ref[...]
        pt = jnp.exp2(st - m_lag).astype(BF16)
        m_new = jnp.maximum(m_lag, jnp.max(st, axis=0, keepdims=True))
        acc = acc_ref[...] + jnp.dot(v_ref[0, j], pt, preferred_element_type=F32)
        acc_ref[...] = acc * jnp.exp2(m_lag - m_new)
        m_ref[...] = m_new

    if nk and not lagged:
        produce(0, sa_ref, xa_ref)

    st = scores(kc_ref[0])
    m0 = jnp.max(st, axis=0, keepdims=True)
    acc_ref[...] = jnp.dot(vc_ref[0, 0], jnp.exp2(st - m0).astype(BF16), preferred_element_type=F32)
    m_ref[...] = m0

    if nk and lagged:
        def body(jj, carry):
            absorb_lagged(2 * jj)
            absorb_lagged(2 * jj + 1)
            return carry

        lax.fori_loop(0, nk // 2, body, 0)
    elif nk:
        def body(jj, carry):
            j = 2 * jj
            produce(j + 1, sb_ref, xb_ref)
            absorb(sa_ref, xa_ref, v_ref[0, j])
            produce(j + 2, sa_ref, xa_ref)
            absorb(sb_ref, xb_ref, v_ref[0, j + 1])
            return carry

        lax.fori_loop(0, nk // 2 - 1, body, 0)
        produce(nk - 1, sb_ref, xb_ref)
        absorb(sa_ref, xa_ref, v_ref[0, nk - 2])
        absorb(sb_ref, xb_ref, v_ref[0, nk - 1])

    acc = acc_ref[...]
    out = acc[0:HEAD_DIM, :] / acc[HEAD_DIM:HEAD_DIM + 1, :]
    tq = o_ref.shape[0]
    for g in range(GROUP):
        o_ref[:, g * HEAD_DIM:(g + 1) * HEAD_DIM] = jnp.transpose(out[:, g * tq:(g + 1) * tq]).astype(o_ref.dtype)


def _flash_call(q, kc, vc, k, v, lagged=False):
    nq, cols = q.shape[1], q.shape[3]
    tq = cols // GROUP
    head3 = lambda h, i: (h, 0, 0)
    head4 = lambda h, i: (h, 0, 0, 0)
    in_specs = [pl.BlockSpec((1, 1, HEAD_DIM, cols), lambda h, i: (h, i, 0, 0)),
                pl.BlockSpec((1,) + kc.shape[1:], head3), pl.BlockSpec((1,) + vc.shape[1:], head4)]
    args = [q, kc, vc]
    scratch = [pltpu.VMEM((1, cols), F32), pltpu.VMEM((ATT_VROWS, cols), F32)]
    nk = tk = 0
    if k is not None:
        nk, tk = v.shape[1], v.shape[3]
        in_specs += [pl.BlockSpec((1,) + k.shape[1:], head3), pl.BlockSpec((1,) + v.shape[1:], head4)]
        args += [k, v]
        assert nk % 2 == 0 and nk >= 2
        if not lagged:
            scratch += [pltpu.VMEM((tk, cols), F32), pltpu.VMEM((tk, cols), F32),
                        pltpu.VMEM((1, cols), F32), pltpu.VMEM((1, cols), F32)]
    return pl.pallas_call(
        functools.partial(_flash_kernel, tk=tk, nk=nk, lagged=lagged),
        grid=(KV_HEADS, nq),
        in_specs=in_specs,
        out_specs=pl.BlockSpec((tq, GROUP * HEAD_DIM), lambda h, i: (i, h)),
        out_shape=jax.ShapeDtypeStruct((nq * tq, Q_HEADS * HEAD_DIM), BF16),
        scratch_shapes=scratch,
        compiler_params=_cparams("arbitrary", "arbitrary"),
        name="gqa_flash_lagged" if lagged else "gqa_flash",
    )(*args)


def _ml_proj_kernel(x_ref, vec_ref, wq_ref, wkt_ref, wv_ref, wg_ref, wgt_ref, bg_ref, bgt_ref,
                    q_ref, kt_ref, v_ref, gc_ref, gr_ref):
    h = _modulated(x_ref[...], vec_ref).astype(BF16)
    nt = (((1,), (1,)), ((), ()))
    q_ref[...] = jnp.dot(h, wq_ref[...], preferred_element_type=F32).astype(BF16)
    kt = lax.dot_general(wkt_ref[...], h, nt, preferred_element_type=F32)
    kt_ref[...] = (kt * (ML_DK ** -0.5)).astype(BF16)
    v_ref[...] = jnp.dot(h, wv_ref[...], preferred_element_type=F32).astype(BF16)
    gc_ref[...] = jnp.dot(h, wg_ref[...], preferred_element_type=F32) + bg_ref[...]
    gr_ref[...] = lax.dot_general(wgt_ref[...], h, nt, preferred_element_type=F32) + bgt_ref[...]


def _ml_proj_call(x, vec, wq, wkt, wv, wg, wgt, bg, bgt, tm):
    t, d = x.shape
    row = lambda i: (i, 0)
    col = lambda i: (0, i)
    return pl.pallas_call(
        _ml_proj_kernel,
        grid=(t // tm,),
        in_specs=[pl.BlockSpec((tm, d), row), _full((8, d)), _full(wq.shape), _full(wkt.shape), _full(wv.shape),
                  _full(wg.shape), _full(wgt.shape), _full(bg.shape), _full(bgt.shape)],
        out_specs=[pl.BlockSpec((tm, ML_KD), row), pl.BlockSpec((ML_KD, tm), col), pl.BlockSpec((tm, d), row),
                   pl.BlockSpec((tm, ML_GATES), row), pl.BlockSpec((ML_GATES, tm), col)],
        out_shape=[jax.ShapeDtypeStruct((t, ML_KD), BF16), jax.ShapeDtypeStruct((ML_KD, t), BF16),
                   jax.ShapeDtypeStruct((t, d), BF16), jax.ShapeDtypeStruct((t, ML_GATES), F32),
                   jax.ShapeDtypeStruct((ML_GATES, t), F32)],
        compiler_params=_cparams("arbitrary"),
        name="mlstm_in_proj",
    )(x, vec, wq, wkt, wv, wg, wgt, bg, bgt)


def _ml_scan_kernel(qf_ref, ktf_ref, vf_ref, gcf_ref, grf_ref, qb_ref, ktb_ref, vb_ref, gcb_ref, grb_ref,
                    c0_ref, m0_ref, hf_ref, hb_ref, cout_ref, mout_ref, c_ref, m_ref):
    n = pl.program_id(0)
    L = ML_CHUNK

    @pl.when(n == 0)
    def _():
        c_ref[...] = c0_ref[...]
        m_ref[...] = m0_ref[...]

    r = lax.broadcasted_iota(jnp.int32, (L, L), 0)
    s_ = lax.broadcasted_iota(jnp.int32, (L, L), 1)
    lower = r >= s_
    upper = r <= s_
    ones_blk = jnp.ones((L, ML_STATE_W - ML_DV), BF16)
    hi = lax.Precision.HIGHEST
    per_dir = ((qf_ref, ktf_ref, vf_ref, gcf_ref, grf_ref, hf_ref, lower, upper, L - 1),
               (qb_ref, ktb_ref, vb_ref, gcb_ref, grb_ref, hb_ref, upper, lower, 0))
    for d, (q_ref, kt_ref, v_ref, gc_ref, gr_ref, h_ref, seen, seen_t, end) in enumerate(per_dir):
        gcol = gc_ref[...]
        grow = gr_ref[...]
        a_col = jnp.dot(seen.astype(F32), _log_sigmoid(gcol), precision=hi, preferred_element_type=F32)
        a_row = jnp.dot(_log_sigmoid(grow), seen_t.astype(F32), precision=hi, preferred_element_type=F32)
        for hd in range(ML_HEADS):
            ji = (2 * d) * ML_HEADS + hd
            jf = (2 * d + 1) * ML_HEADS + hd
            st = d * ML_HEADS + hd
            a_c = a_col[:, jf:jf + 1]
            a_r = a_row[jf:jf + 1, :]
            b_r = grow[ji:ji + 1, :] - a_r
            a_end = a_r[:, end:end + 1]
            m_prev = m_ref[st:st + 1, 0:1]
            dm = jnp.where(seen, a_c + b_r, -jnp.inf)
            inter = a_c + m_prev
            m_t = jnp.maximum(jnp.max(dm, axis=1, keepdims=True), inter)
            w = jnp.exp(dm - m_t)
            qh = q_ref[:, hd * ML_DK:(hd + 1) * ML_DK]
            kth = kt_ref[hd * ML_DK:(hd + 1) * ML_DK, :]
            sw = jnp.dot(qh, kth, preferred_element_type=F32) * w
            sc = jnp.exp(inter - m_t)
            vp = jnp.concatenate([v_ref[:, hd * ML_DV:(hd + 1) * ML_DV], ones_blk], axis=1)
            cst = c_ref[st]
            hx = (jnp.dot(sw.astype(BF16), vp, preferred_element_type=F32)
                  + sc * jnp.dot(qh, cst.astype(BF16), preferred_element_type=F32))
            den = jnp.maximum(jnp.abs(hx[:, ML_DV:]), jnp.exp(-m_t))
            inv = 1.0 / den
            h_ref[:, hd * ML_DV:hd * ML_DV + 128] = hx[:, 0:128] * inv
            h_ref[:, hd * ML_DV + 128:(hd + 1) * ML_DV] = hx[:, 128:ML_DV] * inv
            w_end = a_end + b_r
            m_new = jnp.maximum(a_end + m_prev, jnp.max(w_end, axis=1, keepdims=True))
            e_r = jnp.exp(w_end - m_new)
            decay = jnp.exp(a_end + m_prev - m_new)
            ke = (kth.astype(F32) * e_r).astype(BF16)
            c_ref[st] = decay * cst + jnp.dot(ke, vp, preferred_element_type=F32)
            m_ref[st:st + 1, :] = jnp.broadcast_to(m_new, (1, 128))

    @pl.when(n == pl.num_programs(0) - 1)
    def _():
        cout_ref[...] = c_ref[...]
        mout_ref[...] = m_ref[...]


def _ml_scan_call(q, kt, v, gc, gr, c0, m0):
    t, d = v.shape
    L = ML_CHUNK
    nc = t // L
    fr = lambda n: (n, 0)
    fc = lambda n: (0, n)
    br = lambda n: (nc - 1 - n, 0)
    bc = lambda n: (0, nc - 1 - n)

    def specs(rm, cm):
        return [pl.BlockSpec((L, ML_KD), rm), pl.BlockSpec((ML_KD, L), cm), pl.BlockSpec((L, d), rm),
                pl.BlockSpec((L, ML_GATES), rm), pl.BlockSpec((ML_GATES, L), cm)]

    nst = 2 * ML_HEADS
    return pl.pallas_call(
        _ml_scan_kernel,
        grid=(nc,),
        in_specs=specs(fr, fc) + specs(br, bc) + [_full(c0.shape), _full(m0.shape)],
        out_specs=[pl.BlockSpec((L, d), fr), pl.BlockSpec((L, d), br), _full(c0.shape), _full(m0.shape)],
        out_shape=[jax.ShapeDtypeStruct((t, d), F32), jax.ShapeDtypeStruct((t, d), F32),
                   jax.ShapeDtypeStruct(c0.shape, F32), jax.ShapeDtypeStruct(m0.shape, F32)],
        scratch_shapes=[pltpu.VMEM((nst, ML_DK, ML_STATE_W), F32), pltpu.VMEM((nst, 128), F32)],
        compiler_params=_cparams("arbitrary"),
        name="mlstm_scan",
    )(q, kt, v, gc, gr, q, kt, v, gc, gr, c0, m0)


def _ml_out_kernel(x_ref, hf_ref, hb_ref, vec_ref, ng_ref, woi_ref, wo_ref, o_ref):
    x = x_ref[...]
    h = _modulated(x, vec_ref).astype(BF16)
    o = jnp.dot(h, woi_ref[...], preferred_element_type=F32)
    hs = hf_ref[...] + hb_ref[...]
    parts = []
    for hd in range(ML_HEADS):
        seg = hs[:, hd * ML_DV:(hd + 1) * ML_DV]
        parts.append(_head_rms(seg, ng_ref[0:1, hd * ML_DV:(hd + 1) * ML_DV]))
    hn = jnp.concatenate(parts, axis=1)
    y = jnp.dot((_sigmoid(o) * hn).astype(BF16), wo_ref[...], preferred_element_type=F32)
    o_ref[...] = x + vec_ref[2:3, :] * y


def _ml_out_call(x, hf, hb, vec, ng8, woi, wo, tm):
    t, d = x.shape
    row = lambda i: (i, 0)
    return pl.pallas_call(
        _ml_out_kernel,
        grid=(t // tm,),
        in_specs=[pl.BlockSpec((tm, d), row), pl.BlockSpec((tm, d), row), pl.BlockSpec((tm, d), row),
                  _full((8, d)), _full((8, d)), _full(woi.shape), _full(wo.shape)],
        out_specs=pl.BlockSpec((tm, d), row),
        out_shape=jax.ShapeDtypeStruct((t, d), F32),
        compiler_params=_cparams("arbitrary"),
        name="mlstm_out",
    )(x, hf, hb, vec, ng8, woi, wo)


def _pad8(rows):
    a = jnp.stack(rows)
    return jnp.concatenate([a, jnp.zeros((8 - a.shape[0], a.shape[1]), a.dtype)], axis=0)


def _rope_tables(n_tok):
    rows = n_tok // GRID_W
    seg = HEAD_DIM // 2
    inv = ROPE_THETA ** (-jnp.arange(seg // 2, dtype=F32) / (seg // 2))
    ang_r = jnp.arange(rows).astype(F32)[:, None] * inv
    ang_c = jnp.arange(GRID_W).astype(F32)[:, None] * inv

    def expand(fr, fc):
        r = jnp.repeat(fr, GRID_W, axis=0)
        c = jnp.tile(fc, (rows, 1))
        return jnp.concatenate([r, r, c, c], axis=-1)

    return expand(jnp.cos(ang_r), jnp.cos(ang_c)), expand(jnp.sin(ang_r), jnp.sin(ang_c))


def _ctx_read_at_or_after(i):
    return any((j % N_MIXERS) != 0 for j in range(i, DEPTH))


def kernel(x, c, ctx, c_ctx, mod_w, mod_b, norm_g, ffn_w13, ffn_w2, conv_w_in, conv_k, conv_w_out,
           attn_w_qkv, attn_q_g, attn_k_g, attn_w_o, mlstm_w_in, mlstm_b_gate, mlstm_norm_g, mlstm_w_o):
    d = D_MODEL
    t_lat = x.shape[1]
    t_ctx = ctx.shape[1]
    tm = 512
    tmc = t_ctx
    lat = x[0]
    cx = ctx[0]

    cvec = _pad8([c[0], c_ctx])
    modall = _mod_call(cvec, mod_w, mod_b)

    def vec(i, a, stream):
        m = modall[i, stream]
        base = 3 * a * d
        return _pad8([m[base:base + d], m[base + d:base + 2 * d], m[base + 2 * d:base + 3 * d], norm_g[i, a]])

    cos, sin = _rope_tables(t_lat)
    w13_all, w2_all = ffn_w13, ffn_w2
    conv_in_all, conv_out_all = conv_w_in.astype(BF16), conv_w_out.astype(BF16)
    counters = [0, 0, 0]
    for i in range(DEPTH):
        kind = i % N_MIXERS
        j = counters[kind]
        counters[kind] += 1
        ctx_out = _ctx_read_at_or_after(i + 1)
        if not _ctx_read_at_or_after(i):
            cx = None
        lat = _ffn_call(lat, vec(i, 0, 0), w13_all, w2_all, i, 0, tm)
        if cx is not None:
            cx = _ffn_call(cx, vec(i, 0, 1), w13_all, w2_all, i, 0, tmc)
        vl, vc = vec(i, 1, 0), vec(i, 1, 1)
        if kind == 0:
            k8 = _pad8([conv_k[j, 0], conv_k[j, 1], conv_k[j, 2]])
            lat = _conv_call(lat, vl, conv_in_all, k8, conv_out_all, j, CONV_TM)
            if ctx_out:
                cx = _conv_call(cx, vc, conv_in_all, k8, conv_out_all, j, tmc)
        elif kind == 1:
            w_qkv, w_o = attn_w_qkv[j], attn_w_o[j].astype(BF16)
            qd, kd = Q_HEADS * HEAD_DIM, KV_HEADS * HEAD_DIM
            wqv_t = jnp.concatenate([w_qkv[:, :qd], w_qkv[:, qd + kd:]], axis=1).T.astype(BF16)
            wk = w_qkv[:, qd:qd + kd].astype(BF16)
            gk8 = _pad8([attn_k_g[j]])
            gq = attn_q_g[j][:, None]
            one, zero = jnp.ones((t_ctx, HEAD_DIM), F32), jnp.zeros((t_ctx, HEAD_DIM), F32)
            q, k, v = _qkv_call(lat, vl, wqv_t, wk, jnp.broadcast_to(gq, (HEAD_DIM, tm)), gk8,
                                cos, sin, cos.T, sin.T, tm, ATT_TQ)
            qc, kc, vc_ = _qkv_call(cx, vc, wqv_t, wk, jnp.broadcast_to(gq, (HEAD_DIM, tmc)), gk8,
                                    one, zero, one.T, zero.T, tmc, t_ctx)
            bound = (HEAD_DIM * QK_SCALE_LOG2) * jnp.max(jnp.abs(attn_q_g[j])) * jnp.max(jnp.abs(attn_k_g[j]))
            att = lax.cond(2.0 * bound <= ATT_LAG_MAX_EXP,
                           lambda *a: _flash_call(*a, lagged=True), lambda *a: _flash_call(*a), q, kc, vc_, k, v)
            lat = _proj_res_call(lat, att, vl, w_o, tm)
            if ctx_out:
                attc = _flash_call(qc, kc, vc_, None, None)
                cx = _proj_res_call(cx, attc, vc, w_o, tmc)
        else:
            w_in = mlstm_w_in[j]
            kd = ML_KD
            g0 = kd + d
            q0 = g0 + ML_GATES
            o0 = q0 + kd
            wkt = w_in[:, :kd].T.astype(BF16)
            wv = w_in[:, kd:g0].astype(BF16)
            wg = w_in[:, g0:q0].astype(BF16)
            wq = w_in[:, q0:o0].astype(BF16)
            woi = w_in[:, o0:].astype(BF16)
            bg = mlstm_b_gate[j].reshape(1, ML_GATES)
            pc = _ml_proj_call(cx, vc, wq, wkt, wv, wg, wg.T, bg, bg.T, tmc)
            pl_ = _ml_proj_call(lat, vl, wq, wkt, wv, wg, wg.T, bg, bg.T, tm)
            c0 = jnp.zeros((2 * ML_HEADS, ML_DK, ML_STATE_W), F32)
            m0 = jnp.zeros((2 * ML_HEADS, 128), F32)
            hfc, hbc, c1, m1 = _ml_scan_call(*pc, c0, m0)
            hf, hb, _, _ = _ml_scan_call(*pl_, c1, m1)
            ng8 = _pad8([mlstm_norm_g[j]])
            wo = mlstm_w_o[j].astype(BF16)
            lat = _ml_out_call(lat, hf, hb, vl, ng8, woi, wo, tm)
            if ctx_out:
                cx = _ml_out_call(cx, hfc, hbc, vc, ng8, woi, wo, tmc)
        if ctx_out:
            cx = _ffn_call(cx, vec(i, 2, 1), w13_all, w2_all, i, 1, tmc)
        else:
            cx = None
        lat = _ffn_call(lat, vec(i, 2, 0), w13_all, w2_all, i, 1, tm)
    return lat[None]
```

```python
import functools

import jax
import jax.numpy as jnp
from jax import lax
from jax.experimental import pallas as pl
from jax.experimental.pallas import tpu as pltpu

F32 = jnp.float32
BF16 = jnp.bfloat16

D_MODEL = 1024
DEPTH = 4
N_MIXERS = 3
NORM_EPS = 1e-6
FFN_HIDDEN = 2816
FFN_CHUNK = 256
GRID_W = 64
HEAD_DIM = 128
Q_HEADS = 8
KV_HEADS = 2
GROUP = Q_HEADS // KV_HEADS
ROPE_THETA = 10000.0
ML_HEADS = 4
ML_DV = 256
ML_DK = 128
ML_KD = ML_HEADS * ML_DK
ML_GATES = 4 * ML_HEADS
ML_CHUNK = 256
ML_STATE_W = ML_DV + 128
QK_SCALE_LOG2 = (HEAD_DIM ** -0.5) * 1.4426950408889634

VMEM_LIMIT = 56 * 1024 * 1024


def _cparams(*sem):
    return pltpu.CompilerParams(dimension_semantics=sem, vmem_limit_bytes=VMEM_LIMIT)


def _full(shape):
    return pl.BlockSpec(shape, lambda *_: (0,) * len(shape))


def _modulated(x, vec_ref):
    ms = jnp.mean(x * x, axis=-1, keepdims=True)
    y = x * lax.rsqrt(ms + NORM_EPS) * vec_ref[3:4, :]
    return y * (1.0 + vec_ref[1:2, :]) + vec_ref[0:1, :]


def _sigmoid(x):
    return 1.0 / (1.0 + jnp.exp(-x))


def _log_sigmoid(x):
    return jnp.minimum(x, 0.0) - jnp.log(1.0 + jnp.exp(-jnp.abs(x)))


def _mod_kernel(c_ref, w_ref, b_ref, o_ref):
    c = c_ref[...]
    s = (c * _sigmoid(c)).astype(BF16)
    o_ref[0] = jnp.dot(s, w_ref[0].astype(BF16), preferred_element_type=F32) + b_ref[0]


def _mod_call(cvec, mod_w, mod_b):
    depth, d, n = mod_w.shape
    tn = 1024
    return pl.pallas_call(
        _mod_kernel,
        grid=(depth, n // tn),
        in_specs=[_full((8, d)),
                  pl.BlockSpec((1, d, tn), lambda l, j: (l, 0, j)),
                  pl.BlockSpec((1, 1, tn), lambda l, j: (l, 0, j))],
        out_specs=pl.BlockSpec((1, 8, tn), lambda l, j: (l, 0, j)),
        out_shape=jax.ShapeDtypeStruct((depth, 8, n), F32),
        compiler_params=_cparams("arbitrary", "arbitrary"),
        name="adaln_mod",
    )(cvec, mod_w, mod_b.reshape(depth, 1, n))


def _ffn_kernel(x_ref, vec_ref, w13_ref, w2_ref, o_ref, acc_ref):
    x = x_ref[...]
    h = _modulated(x, vec_ref).astype(BF16)
    for j in range(FFN_HIDDEN // FFN_CHUNK):
        lo = j * FFN_CHUNK
        hi = FFN_HIDDEN + lo
        a = jnp.dot(h, w13_ref[:, lo:lo + FFN_CHUNK].astype(BF16), preferred_element_type=F32)
        b = jnp.dot(h, w13_ref[:, hi:hi + FFN_CHUNK].astype(BF16), preferred_element_type=F32)
        u = (a * _sigmoid(a) * b).astype(BF16)
        y = jnp.dot(u, w2_ref[lo:lo + FFN_CHUNK, :].astype(BF16), preferred_element_type=F32)
        if j == 0:
            acc_ref[...] = y
        else:
            acc_ref[...] += y
    o_ref[...] = x + (0.5 * vec_ref[2:3, :]) * acc_ref[...]


def _ffn_call(x, vec, w13_all, w2_all, layer, which, tm):
    t, d = x.shape
    pick = lambda i: (layer, which, 0, 0)
    return pl.pallas_call(
        _ffn_kernel,
        grid=(t // tm,),
        in_specs=[pl.BlockSpec((tm, d), lambda i: (i, 0)),
                  _full((8, d)),
                  pl.BlockSpec((None, None) + w13_all.shape[2:], pick, pipeline_mode=pl.Buffered(1)),
                  pl.BlockSpec((None, None) + w2_all.shape[2:], pick, pipeline_mode=pl.Buffered(1))],
        out_specs=pl.BlockSpec((tm, d), lambda i: (i, 0)),
        out_shape=jax.ShapeDtypeStruct((t, d), F32),
        scratch_shapes=[pltpu.VMEM((tm, d), F32)],
        compiler_params=_cparams("arbitrary"),
        name="swiglu_half_step",
    )(x, vec, w13_all, w2_all)


def _proj_res_kernel(x_ref, a_ref, vec_ref, w_ref, o_ref):
    y = jnp.dot(a_ref[...], w_ref[...], preferred_element_type=F32)
    o_ref[...] = x_ref[...] + vec_ref[2:3, :] * y


def _proj_res_call(x, a, vec, w, tm):
    t, d = x.shape
    return pl.pallas_call(
        _proj_res_kernel,
        grid=(t // tm,),
        in_specs=[pl.BlockSpec((tm, d), lambda i: (i, 0)),
                  pl.BlockSpec((tm, a.shape[1]), lambda i: (i, 0)),
                  _full((8, d)),
                  _full(w.shape)],
        out_specs=pl.BlockSpec((tm, d), lambda i: (i, 0)),
        out_shape=jax.ShapeDtypeStruct((t, d), F32),
        compiler_params=_cparams("arbitrary"),
        name="out_proj_residual",
    )(x, a, vec, w)


CONV_HALO = 16
CONV_TM = 1024


def _conv_kernel(x_ref, xp_ref, xn_ref, vec_ref, win_ref, k_ref, wout_ref, o_ref, h_ref, acc_ref, *, tm):
    i = pl.program_id(0)
    last = pl.num_programs(0) - 1
    x = x_ref[...]
    d = x.shape[1]
    keep_prev = (i > 0).astype(F32)
    keep_next = (i < last).astype(F32)
    h_ref[0:CONV_HALO, :] = (_modulated(xp_ref[...], vec_ref) * keep_prev).astype(BF16)
    h_ref[CONV_HALO:CONV_HALO + tm, :] = _modulated(x, vec_ref).astype(BF16)
    h_ref[CONV_HALO + tm:, :] = (_modulated(xn_ref[...], vec_ref) * keep_next).astype(BF16)
    rows = tm + 2 * CONV_HALO
    cw = 256
    for j in range(d // cw):
        lo = j * cw
        hall = h_ref[...]
        cg = jnp.dot(hall, win_ref[:, d + lo:d + lo + cw], preferred_element_type=F32)
        xv = jnp.dot(hall, win_ref[:, 2 * d + lo:2 * d + lo + cw], preferred_element_type=F32)
        u = cg * xv
        up = pltpu.roll(u, 1, axis=0)
        un = pltpu.roll(u, rows - 1, axis=0)
        kk = k_ref[:, lo:lo + cw]
        conv = kk[0:1, :] * up + kk[1:2, :] * u + kk[2:3, :] * un
        conv = conv[CONV_HALO:CONV_HALO + tm, :]
        bg = jnp.dot(h_ref[CONV_HALO:CONV_HALO + tm, :], win_ref[:, lo:lo + cw], preferred_element_type=F32)
        y = jnp.dot((bg * conv).astype(BF16), wout_ref[lo:lo + cw, :], preferred_element_type=F32)
        if j == 0:
            acc_ref[...] = y
        else:
            acc_ref[...] += y
    o_ref[...] = x + vec_ref[2:3, :] * acc_ref[...]


def _conv_call(x, vec, w_in_all, k8, w_out_all, j, tm):
    t, d = x.shape
    hb = tm // CONV_HALO
    nhb = t // CONV_HALO
    pick = lambda i: (j, 0, 0)
    return pl.pallas_call(
        functools.partial(_conv_kernel, tm=tm),
        grid=(t // tm,),
        in_specs=[pl.BlockSpec((tm, d), lambda i: (i, 0)),
                  pl.BlockSpec((CONV_HALO, d), lambda i: (jnp.maximum(i * hb - 1, 0), 0)),
                  pl.BlockSpec((CONV_HALO, d), lambda i: (jnp.minimum((i + 1) * hb, nhb - 1), 0)),
                  _full((8, d)),
                  pl.BlockSpec((None,) + w_in_all.shape[1:], pick),
                  _full((8, d)),
                  pl.BlockSpec((None,) + w_out_all.shape[1:], pick)],
        out_specs=pl.BlockSpec((tm, d), lambda i: (i, 0)),
        out_shape=jax.ShapeDtypeStruct((t, d), F32),
        scratch_shapes=[pltpu.VMEM((tm + 2 * CONV_HALO, d), BF16), pltpu.VMEM((tm, d), F32)],
        compiler_params=_cparams("arbitrary"),
        name="short_conv_mixer",
    )(x, x, x, vec, w_in_all, k8, w_out_all)


def _head_rms(t, g):
    return t * lax.rsqrt(jnp.mean(t * t, axis=-1, keepdims=True) + NORM_EPS) * g


def _rope(t, cos, sin):
    lane = lax.broadcasted_iota(jnp.int32, t.shape, 1)
    first = (lane % (HEAD_DIM // 2)) < (HEAD_DIM // 4)
    rot = jnp.where(first, -pltpu.roll(t, HEAD_DIM - HEAD_DIM // 4, axis=1), pltpu.roll(t, HEAD_DIM // 4, axis=1))
    return t * cos + rot * sin


ATT_TQ = 256
ATT_LAG_MAX_EXP = 64.0
ATT_VROWS = HEAD_DIM + 16


def _qkv_kernel(x_ref, vec_ref, wqv_ref, wk_ref, gq_ref, gk_ref, cos_ref, sin_ref, cost_ref, sint_ref,
                q_ref, k_ref, v_ref, *, tq):
    h = _modulated(x_ref[...], vec_ref).astype(BF16)
    tm = h.shape[0]
    pt = lax.dot_general(wqv_ref[...], h, (((1,), (1,)), ((), ())), preferred_element_type=F32)
    pk = jnp.dot(h, wk_ref[...], preferred_element_type=F32)
    cost = cost_ref[...]
    sint = sint_ref[...]
    gq = gq_ref[...]
    qr = HEAD_DIM // 4
    for hq in range(Q_HEADS):
        t = pt[hq * HEAD_DIM:(hq + 1) * HEAD_DIM, :]
        t = t * lax.rsqrt(jnp.mean(t * t, axis=0, keepdims=True) + NORM_EPS) * gq
        rot = jnp.concatenate([-t[qr:2 * qr], t[0:qr], -t[3 * qr:4 * qr], t[2 * qr:3 * qr]], axis=0)
        qt = ((t * cost + rot * sint) * QK_SCALE_LOG2).astype(BF16)
        g = hq % GROUP
        for sub in range(tm // tq):
            q_ref[hq // GROUP, sub, :, g * tq:(g + 1) * tq] = qt[:, sub * tq:(sub + 1) * tq]
    qd = Q_HEADS * HEAD_DIM
    cos = cos_ref[...]
    sin = sin_ref[...]
    for hk in range(KV_HEADS):
        t = _head_rms(pk[:, hk * HEAD_DIM:(hk + 1) * HEAD_DIM], gk_ref[0:1, :])
        k_ref[hk] = _rope(t, cos, sin).astype(BF16)
        v_ref[hk, 0, 0:HEAD_DIM, :] = pt[qd + hk * HEAD_DIM:qd + (hk + 1) * HEAD_DIM, :].astype(BF16)
        v_ref[hk, 0, HEAD_DIM:, :] = jnp.ones((ATT_VROWS - HEAD_DIM, tm), BF16)


def _qkv_call(x, vec, wqv_t, wk, gq_b, gk8, cos, sin, cos_t, sin_t, tm, tq):
    t, d = x.shape
    nsub = tm // tq
    row = lambda i: (i, 0)
    col = lambda i: (0, i)
    return pl.pallas_call(
        functools.partial(_qkv_kernel, tq=tq),
        grid=(t // tm,),
        in_specs=[pl.BlockSpec((tm, d), row),
                  _full((8, d)),
                  _full(wqv_t.shape),
                  _full(wk.shape),
                  _full((HEAD_DIM, tm)),
                  _full((8, HEAD_DIM)),
                  pl.BlockSpec((tm, HEAD_DIM), row),
                  pl.BlockSpec((tm, HEAD_DIM), row),
                  pl.BlockSpec((HEAD_DIM, tm), col),
                  pl.BlockSpec((HEAD_DIM, tm), col)],
        out_specs=[pl.BlockSpec((KV_HEADS, nsub, HEAD_DIM, GROUP * tq), lambda i: (0, i, 0, 0)),
                   pl.BlockSpec((KV_HEADS, tm, HEAD_DIM), lambda i: (0, i, 0)),
                   pl.BlockSpec((KV_HEADS, 1, ATT_VROWS, tm), lambda i: (0, i, 0, 0))],
        out_shape=[jax.ShapeDtypeStruct((KV_HEADS, t // tq, HEAD_DIM, GROUP * tq), BF16),
                   jax.ShapeDtypeStruct((KV_HEADS, t, HEAD_DIM), BF16),
                   jax.ShapeDtypeStruct((KV_HEADS, t // tm, ATT_VROWS, tm), BF16)],
        compiler_params=_cparams("arbitrary"),
        name="gqa_qkv_proj",
    )(x, vec, wqv_t, wk, gq_b, gk8, cos, sin, cos_t, sin_t)


def _flash_kernel(*refs, tk, nk, lagged):
    if nk and not lagged:
        q_ref, kc_ref, vc_ref, k_ref, v_ref, o_ref, m_ref, acc_ref, sa_ref, sb_ref, xa_ref, xb_ref = refs
    elif nk:
        q_ref, kc_ref, vc_ref, k_ref, v_ref, o_ref, m_ref, acc_ref = refs
    else:
        q_ref, kc_ref, vc_ref, o_ref, m_ref, acc_ref = refs
    qt = q_ref[0, 0]

    def scores(ks):
        return jnp.dot(ks, qt, preferred_element_type=F32)

    def key_tile(j):
        return k_ref[0, pl.ds(pl.multiple_of(j * tk, tk), tk), :]

    def produce(j, s_ref, x_ref):
        st = scores(key_tile(j))
        s_ref[...] = st
        x_ref[...] = jnp.max(st, axis=0, keepdims=True)

    def absorb(s_ref, x_ref, vt):
        m_old = m_ref[...]
        m_new = jnp.maximum(m_old, x_ref[...])
        alpha = jnp.exp2(m_old - m_new)
        pt = jnp.exp2(s_ref[...] - m_new).astype(BF16)
        acc_ref[...] = alpha * acc_ref[...] + jnp.dot(vt, pt, preferred_element_type=F32)
        m_ref[...] = m_new

    def absorb_lagged(j):
        st = scores(key_tile(j))
        m_lag = m_ref[...]
        pt = jnp.exp2(st - m_lag).astype(BF16)
        m_new = jnp.maximum(m_lag, jnp.max(st, axis=0, keepdims=True))
        acc = acc_ref[...] + jnp.dot(v_ref[0, j], pt, preferred_element_type=F32)
        acc_ref[...] = acc * jnp.exp2(m_lag - m_new)
        m_ref[...] = m_new

    if nk and not lagged:
        produce(0, sa_ref, xa_ref)

    st = scores(kc_ref[0])
    m0 = jnp.max(st, axis=0, keepdims=True)
    acc_ref[...] = jnp.dot(vc_ref[0, 0], jnp.exp2(st - m0).astype(BF16), preferred_element_type=F32)
    m_ref[...] = m0

    if nk and lagged:
        def body(jj, carry):
            absorb_lagged(2 * jj)
            absorb_lagged(2 * jj + 1)
            return carry

        lax.fori_loop(0, nk // 2, body, 0)
    elif nk:
        def body(jj, carry):
            j = 2 * jj
            produce(j + 1, sb_ref, xb_ref)
            absorb(sa_ref, xa_ref, v_ref[0, j])
            produce(j + 2, sa_ref, xa_ref)
            absorb(sb_ref, xb_ref, v_ref[0, j + 1])
            return carry

        lax.fori_loop(0, nk // 2 - 1, body, 0)
        produce(nk - 1, sb_ref, xb_ref)
        absorb(sa_ref, xa_ref, v_ref[0, nk - 2])
        absorb(sb_ref, xb_ref, v_ref[0, nk - 1])

    acc = acc_ref[...]
    out = acc[0:HEAD_DIM, :] / acc[HEAD_DIM:HEAD_DIM + 1, :]
    tq = o_ref.shape[0]
    for g in range(GROUP):
        o_ref[:, g * HEAD_DIM:(g + 1) * HEAD_DIM] = jnp.transpose(out[:, g * tq:(g + 1) * tq]).astype(o_ref.dtype)


def _flash_call(q, kc, vc, k, v, lagged=False):
    nq, cols = q.shape[1], q.shape[3]
    tq = cols // GROUP
    head3 = lambda h, i: (h, 0, 0)
    head4 = lambda h, i: (h, 0, 0, 0)
    in_specs = [pl.BlockSpec((1, 1, HEAD_DIM, cols), lambda h, i: (h, i, 0, 0)),
                pl.BlockSpec((1,) + kc.shape[1:], head3), pl.BlockSpec((1,) + vc.shape[1:], head4)]
    args = [q, kc, vc]
    scratch = [pltpu.VMEM((1, cols), F32), pltpu.VMEM((ATT_VROWS, cols), F32)]
    nk = tk = 0
    if k is not None:
        nk, tk = v.shape[1], v.shape[3]
        in_specs += [pl.BlockSpec((1,) + k.shape[1:], head3), pl.BlockSpec((1,) + v.shape[1:], head4)]
        args += [k, v]
        assert nk % 2 == 0 and nk >= 2
        if not lagged:
            scratch += [pltpu.VMEM((tk, cols), F32), pltpu.VMEM((tk, cols), F32),
                        pltpu.VMEM((1, cols), F32), pltpu.VMEM((1, cols), F32)]
    return pl.pallas_call(
        functools.partial(_flash_kernel, tk=tk, nk=nk, lagged=lagged),
        grid=(KV_HEADS, nq),
        in_specs=in_specs,
        out_specs=pl.BlockSpec((tq, GROUP * HEAD_DIM), lambda h, i: (i, h)),
        out_shape=jax.ShapeDtypeStruct((nq * tq, Q_HEADS * HEAD_DIM), BF16),
        scratch_shapes=scratch,
        compiler_params=_cparams("arbitrary", "arbitrary"),
        name="gqa_flash_lagged" if lagged else "gqa_flash",
    )(*args)


def _ml_proj_kernel(x_ref, vec_ref, wq_ref, wkt_ref, wv_ref, wg_ref, wgt_ref, bg_ref, bgt_ref,
                    q_ref, kt_ref, v_ref, gc_ref, gr_ref):
    h = _modulated(x_ref[...], vec_ref).astype(BF16)
    nt = (((1,), (1,)), ((), ()))
    q_ref[...] = jnp.dot(h, wq_ref[...], preferred_element_type=F32).astype(BF16)
    kt = lax.dot_general(wkt_ref[...], h, nt, preferred_element_type=F32)
    kt_ref[...] = (kt * (ML_DK ** -0.5)).astype(BF16)
    v_ref[...] = jnp.dot(h, wv_ref[...], preferred_element_type=F32).astype(BF16)
    gc_ref[...] = jnp.dot(h, wg_ref[...], preferred_element_type=F32) + bg_ref[...]
    gr_ref[...] = lax.dot_general(wgt_ref[...], h, nt, preferred_element_type=F32) + bgt_ref[...]


def _ml_proj_call(x, vec, wq, wkt, wv, wg, wgt, bg, bgt, tm):
    t, d = x.shape
    row = lambda i: (i, 0)
    col = lambda i: (0, i)
    return pl.pallas_call(
        _ml_proj_kernel,
        grid=(t // tm,),
        in_specs=[pl.BlockSpec((tm, d), row), _full((8, d)), _full(wq.shape), _full(wkt.shape), _full(wv.shape),
                  _full(wg.shape), _full(wgt.shape), _full(bg.shape), _full(bgt.shape)],
        out_specs=[pl.BlockSpec((tm, ML_KD), row), pl.BlockSpec((ML_KD, tm), col), pl.BlockSpec((tm, d), row),
                   pl.BlockSpec((tm, ML_GATES), row), pl.BlockSpec((ML_GATES, tm), col)],
        out_shape=[jax.ShapeDtypeStruct((t, ML_KD), BF16), jax.ShapeDtypeStruct((ML_KD, t), BF16),
                   jax.ShapeDtypeStruct((t, d), BF16), jax.ShapeDtypeStruct((t, ML_GATES), F32),
                   jax.ShapeDtypeStruct((ML_GATES, t), F32)],
        compiler_params=_cparams("arbitrary"),
        name="mlstm_in_proj",
    )(x, vec, wq, wkt, wv, wg, wgt, bg, bgt)


def _ml_scan_kernel(qf_ref, ktf_ref, vf_ref, gcf_ref, grf_ref, qb_ref, ktb_ref, vb_ref, gcb_ref, grb_ref,
                    c0_ref, m0_ref, hf_ref, hb_ref, cout_ref, mout_ref, c_ref, m_ref):
    n = pl.program_id(0)
    L = ML_CHUNK

    @pl.when(n == 0)
    def _():
        c_ref[...] = c0_ref[...]
        m_ref[...] = m0_ref[...]

    r = lax.broadcasted_iota(jnp.int32, (L, L), 0)
    s_ = lax.broadcasted_iota(jnp.int32, (L, L), 1)
    lower = r >= s_
    upper = r <= s_
    ones_blk = jnp.ones((L, ML_STATE_W - ML_DV), BF16)
    hi = lax.Precision.HIGHEST
    per_dir = ((qf_ref, ktf_ref, vf_ref, gcf_ref, grf_ref, hf_ref, lower, upper, L - 1),
               (qb_ref, ktb_ref, vb_ref, gcb_ref, grb_ref, hb_ref, upper, lower, 0))
    for d, (q_ref, kt_ref, v_ref, gc_ref, gr_ref, h_ref, seen, seen_t, end) in enumerate(per_dir):
        gcol = gc_ref[...]
        grow = gr_ref[...]
        a_col = jnp.dot(seen.astype(F32), _log_sigmoid(gcol), precision=hi, preferred_element_type=F32)
        a_row = jnp.dot(_log_sigmoid(grow), seen_t.astype(F32), precision=hi, preferred_element_type=F32)
        for hd in range(ML_HEADS):
            ji = (2 * d) * ML_HEADS + hd
            jf = (2 * d + 1) * ML_HEADS + hd
            st = d * ML_HEADS + hd
            a_c = a_col[:, jf:jf + 1]
            a_r = a_row[jf:jf + 1, :]
            b_r = grow[ji:ji + 1, :] - a_r
            a_end = a_r[:, end:end + 1]
            m_prev = m_ref[st:st + 1, 0:1]
            dm = jnp.where(seen, a_c + b_r, -jnp.inf)
            inter = a_c + m_prev
            m_t = jnp.maximum(jnp.max(dm, axis=1, keepdims=True), inter)
            w = jnp.exp(dm - m_t)
            qh = q_ref[:, hd * ML_DK:(hd + 1) * ML_DK]
            kth = kt_ref[hd * ML_DK:(hd + 1) * ML_DK, :]
            sw = jnp.dot(qh, kth, preferred_element_type=F32) * w
            sc = jnp.exp(inter - m_t)
            vp = jnp.concatenate([v_ref[:, hd * ML_DV:(hd + 1) * ML_DV], ones_blk], axis=1)
            cst = c_ref[st]
            hx = (jnp.dot(sw.astype(BF16), vp, preferred_element_type=F32)
                  + sc * jnp.dot(qh, cst.astype(BF16), preferred_element_type=F32))
            den = jnp.maximum(jnp.abs(hx[:, ML_DV:]), jnp.exp(-m_t))
            inv = 1.0 / den
            h_ref[:, hd * ML_DV:hd * ML_DV + 128] = hx[:, 0:128] * inv
            h_ref[:, hd * ML_DV + 128:(hd + 1) * ML_DV] = hx[:, 128:ML_DV] * inv
            w_end = a_end + b_r
            m_new = jnp.maximum(a_end + m_prev, jnp.max(w_end, axis=1, keepdims=True))
            e_r = jnp.exp(w_end - m_new)
            decay = jnp.exp(a_end + m_prev - m_new)
            ke = (kth.astype(F32) * e_r).astype(BF16)
            c_ref[st] = decay * cst + jnp.dot(ke, vp, preferred_element_type=F32)
            m_ref[st:st + 1, :] = jnp.broadcast_to(m_new, (1, 128))

    @pl.when(n == pl.num_programs(0) - 1)
    def _():
        cout_ref[...] = c_ref[...]
        mout_ref[...] = m_ref[...]


def _ml_scan_call(q, kt, v, gc, gr, c0, m0):
    t, d = v.shape
    L = ML_CHUNK
    nc = t // L
    fr = lambda n: (n, 0)
    fc = lambda n: (0, n)
    br = lambda n: (nc - 1 - n, 0)
    bc = lambda n: (0, nc - 1 - n)

    def specs(rm, cm):
        return [pl.BlockSpec((L, ML_KD), rm), pl.BlockSpec((ML_KD, L), cm), pl.BlockSpec((L, d), rm),
                pl.BlockSpec((L, ML_GATES), rm), pl.BlockSpec((ML_GATES, L), cm)]

    nst = 2 * ML_HEADS
    return pl.pallas_call(
        _ml_scan_kernel,
        grid=(nc,),
        in_specs=specs(fr, fc) + specs(br, bc) + [_full(c0.shape), _full(m0.shape)],
        out_specs=[pl.BlockSpec((L, d), fr), pl.BlockSpec((L, d), br), _full(c0.shape), _full(m0.shape)],
        out_shape=[jax.ShapeDtypeStruct((t, d), F32), jax.ShapeDtypeStruct((t, d), F32),
                   jax.ShapeDtypeStruct(c0.shape, F32), jax.ShapeDtypeStruct(m0.shape, F32)],
        scratch_shapes=[pltpu.VMEM((nst, ML_DK, ML_STATE_W), F32), pltpu.VMEM((nst, 128), F32)],
        compiler_params=_cparams("arbitrary"),
        name="mlstm_scan",
    )(q, kt, v, gc, gr, q, kt, v, gc, gr, c0, m0)


def _ml_out_kernel(x_ref, hf_ref, hb_ref, vec_ref, ng_ref, woi_ref, wo_ref, o_ref):
    x = x_ref[...]
    h = _modulated(x, vec_ref).astype(BF16)
    o = jnp.dot(h, woi_ref[...], preferred_element_type=F32)
    hs = hf_ref[...] + hb_ref[...]
    parts = []
    for hd in range(ML_HEADS):
        seg = hs[:, hd * ML_DV:(hd + 1) * ML_DV]
        parts.append(_head_rms(seg, ng_ref[0:1, hd * ML_DV:(hd + 1) * ML_DV]))
    hn = jnp.concatenate(parts, axis=1)
    y = jnp.dot((_sigmoid(o) * hn).astype(BF16), wo_ref[...], preferred_element_type=F32)
    o_ref[...] = x + vec_ref[2:3, :] * y


def _ml_out_call(x, hf, hb, vec, ng8, woi, wo, tm):
    t, d = x.shape
    row = lambda i: (i, 0)
    return pl.pallas_call(
        _ml_out_kernel,
        grid=(t // tm,),
        in_specs=[pl.BlockSpec((tm, d), row), pl.BlockSpec((tm, d), row), pl.BlockSpec((tm, d), row),
                  _full((8, d)), _full((8, d)), _full(woi.shape), _full(wo.shape)],
        out_specs=pl.BlockSpec((tm, d), row),
        out_shape=jax.ShapeDtypeStruct((t, d), F32),
        compiler_params=_cparams("arbitrary"),
        name="mlstm_out",
    )(x, hf, hb, vec, ng8, woi, wo)


def _pad8(rows):
    a = jnp.stack(rows)
    return jnp.concatenate([a, jnp.zeros((8 - a.shape[0], a.shape[1]), a.dtype)], axis=0)


def _rope_tables(n_tok):
    rows = n_tok // GRID_W
    seg = HEAD_DIM // 2
    inv = ROPE_THETA ** (-jnp.arange(seg // 2, dtype=F32) / (seg // 2))
    ang_r = jnp.arange(rows).astype(F32)[:, None] * inv
    ang_c = jnp.arange(GRID_W).astype(F32)[:, None] * inv

    def expand(fr, fc):
        r = jnp.repeat(fr, GRID_W, axis=0)
        c = jnp.tile(fc, (rows, 1))
        return jnp.concatenate([r, r, c, c], axis=-1)

    return expand(jnp.cos(ang_r), jnp.cos(ang_c)), expand(jnp.sin(ang_r), jnp.sin(ang_c))


def _ctx_read_at_or_after(i):
    return any((j % N_MIXERS) != 0 for j in range(i, DEPTH))


def kernel(x, c, ctx, c_ctx, mod_w, mod_b, norm_g, ffn_w13, ffn_w2, conv_w_in, conv_k, conv_w_out,
           attn_w_qkv, attn_q_g, attn_k_g, attn_w_o, mlstm_w_in, mlstm_b_gate, mlstm_norm_g, mlstm_w_o):
    d = D_MODEL
    t_lat = x.shape[1]
    t_ctx = ctx.shape[1]
    tm = 512
    tmc = t_ctx
    lat = x[0]
    cx = ctx[0]

    cvec = _pad8([c[0], c_ctx])
    modall = _mod_call(cvec, mod_w, mod_b)

    m5 = jnp.transpose(modall[:, :2, :].reshape(DEPTH, 2, 3, 3, d), (0, 2, 1, 3, 4))
    g5 = jnp.broadcast_to(norm_g[:, :, None, None, :], (DEPTH, 3, 2, 1, d))
    vecs_all = jnp.concatenate([m5, g5, jnp.zeros((DEPTH, 3, 2, 4, d), F32)], axis=3)

    def vec(i, a, stream):
        return vecs_all[i, a, stream]

    cos, sin = _rope_tables(t_lat)
    w13_all, w2_all = ffn_w13, ffn_w2
    conv_in_all, conv_out_all = conv_w_in.astype(BF16), conv_w_out.astype(BF16)
    counters = [0, 0, 0]
    for i in range(DEPTH):
        kind = i % N_MIXERS
        j = counters[kind]
        counters[kind] += 1
        ctx_out = _ctx_read_at_or_after(i + 1)
        if not _ctx_read_at_or_after(i):
            cx = None
        lat = _ffn_call(lat, vec(i, 0, 0), w13_all, w2_all, i, 0, tm)
        if cx is not None:
            cx = _ffn_call(cx, vec(i, 0, 1), w13_all, w2_all, i, 0, tmc)
        vl, vc = vec(i, 1, 0), vec(i, 1, 1)
        if kind == 0:
            k8 = _pad8([conv_k[j, 0], conv_k[j, 1], conv_k[j, 2]])
            lat = _conv_call(lat, vl, conv_in_all, k8, conv_out_all, j, CONV_TM)
            if ctx_out:
                cx = _conv_call(cx, vc, conv_in_all, k8, conv_out_all, j, tmc)
        elif kind == 1:
            w_qkv, w_o = attn_w_qkv[j], attn_w_o[j].astype(BF16)
            qd, kd = Q_HEADS * HEAD_DIM, KV_HEADS * HEAD_DIM
            wqv_t = jnp.concatenate([w_qkv[:, :qd], w_qkv[:, qd + kd:]], axis=1).T.astype(BF16)
            wk = w_qkv[:, qd:qd + kd].astype(BF16)
            gk8 = _pad8([attn_k_g[j]])
            gq = attn_q_g[j][:, None]
            one, zero = jnp.ones((t_ctx, HEAD_DIM), F32), jnp.zeros((t_ctx, HEAD_DIM), F32)
            q, k, v = _qkv_call(lat, vl, wqv_t, wk, jnp.broadcast_to(gq, (HEAD_DIM, tm)), gk8,
                                cos, sin, cos.T, sin.T, tm, ATT_TQ)
            qc, kc, vc_ = _qkv_call(cx, vc, wqv_t, wk, jnp.broadcast_to(gq, (HEAD_DIM, tmc)), gk8,
                                    one, zero, one.T, zero.T, tmc, t_ctx)
            bound = (HEAD_DIM * QK_SCALE_LOG2) * jnp.max(jnp.abs(attn_q_g[j])) * jnp.max(jnp.abs(attn_k_g[j]))
            att = lax.cond(2.0 * bound <= ATT_LAG_MAX_EXP,
                           lambda *a: _flash_call(*a, lagged=True), lambda *a: _flash_call(*a), q, kc, vc_, k, v)
            lat = _proj_res_call(lat, att, vl, w_o, tm)
            if ctx_out:
                attc = _flash_call(qc, kc, vc_, None, None)
                cx = _proj_res_call(cx, attc, vc, w_o, tmc)
        else:
            w_in = mlstm_w_in[j]
            kd = ML_KD
            g0 = kd + d
            q0 = g0 + ML_GATES
            o0 = q0 + kd
            wkt = w_in[:, :kd].T.astype(BF16)
            wv = w_in[:, kd:g0].astype(BF16)
            wg = w_in[:, g0:q0].astype(BF16)
            wq = w_in[:, q0:o0].astype(BF16)
            woi = w_in[:, o0:].astype(BF16)
            bg = mlstm_b_gate[j].reshape(1, ML_GATES)
            pc = _ml_proj_call(cx, vc, wq, wkt, wv, wg, wg.T, bg, bg.T, tmc)
            pl_ = _ml_proj_call(lat, vl, wq, wkt, wv, wg, wg.T, bg, bg.T, tm)
            c0 = jnp.zeros((2 * ML_HEADS, ML_DK, ML_STATE_W), F32)
            m0 = jnp.zeros((2 * ML_HEADS, 128), F32)
            hfc, hbc, c1, m1 = _ml_scan_call(*pc, c0, m0)
            hf, hb, _, _ = _ml_scan_call(*pl_, c1, m1)
            ng8 = _pad8([mlstm_norm_g[j]])
            wo = mlstm_w_o[j].astype(BF16)
            lat = _ml_out_call(lat, hf, hb, vl, ng8, woi, wo, tm)
            if ctx_out:
                cx = _ml_out_call(cx, hfc, hbc, vc, ng8, woi, wo, tmc)
        if ctx_out:
            cx = _ffn_call(cx, vec(i, 2, 1), w13_all, w2_all, i, 1, tmc)
        else:
            cx = None
        lat = _ffn_call(lat, vec(i, 2, 0), w13_all, w2_all, i, 1, tm)
    return lat[None]
```

```python
import functools

import jax
import jax.numpy as jnp
from jax import lax
from jax.experimental import pallas as pl
from jax.experimental.pallas import tpu as pltpu

F32 = jnp.float32
BF16 = jnp.bfloat16

D_MODEL = 1024
DEPTH = 4
N_MIXERS = 3
NORM_EPS = 1e-6
FFN_HIDDEN = 2816
FFN_CHUNK = 256
GRID_W = 64
HEAD_DIM = 128
Q_HEADS = 8
KV_HEADS = 2
GROUP = Q_HEADS // KV_HEADS
ROPE_THETA = 10000.0
ML_HEADS = 4
ML_DV = 256
ML_DK = 128
ML_KD = ML_HEADS * ML_DK
ML_GATES = 4 * ML_HEADS
ML_CHUNK = 256
ML_STATE_W = ML_DV + 128
QK_SCALE_LOG2 = (HEAD_DIM ** -0.5) * 1.4426950408889634

VMEM_LIMIT = 56 * 1024 * 1024


def _cparams(*sem):
    return pltpu.CompilerParams(dimension_semantics=sem, vmem_limit_bytes=VMEM_LIMIT)


def _full(shape):
    return pl.BlockSpec(shape, lambda *_: (0,) * len(shape))


def _modulated(x, vec_ref):
    ms = jnp.mean(x * x, axis=-1, keepdims=True)
    y = x * lax.rsqrt(ms + NORM_EPS) * vec_ref[3:4, :]
    return y * (1.0 + vec_ref[1:2, :]) + vec_ref[0:1, :]


def _sigmoid(x):
    return 1.0 / (1.0 + jnp.exp(-x))


def _log_sigmoid(x):
    return jnp.minimum(x, 0.0) - jnp.log(1.0 + jnp.exp(-jnp.abs(x)))


def _mod_kernel(c_ref, w_ref, b_ref, o_ref):
    c = c_ref[...]
    s = (c * _sigmoid(c)).astype(BF16)
    o_ref[0] = jnp.dot(s, w_ref[0].astype(BF16), preferred_element_type=F32) + b_ref[0]


def _mod_call(cvec, mod_w, mod_b):
    depth, d, n = mod_w.shape
    tn = 1024
    return pl.pallas_call(
        _mod_kernel,
        grid=(depth, n // tn),
        in_specs=[_full((8, d)),
                  pl.BlockSpec((1, d, tn), lambda l, j: (l, 0, j)),
                  pl.BlockSpec((1, 1, tn), lambda l, j: (l, 0, j))],
        out_specs=pl.BlockSpec((1, 8, tn), lambda l, j: (l, 0, j)),
        out_shape=jax.ShapeDtypeStruct((depth, 8, n), F32),
        compiler_params=_cparams("arbitrary", "arbitrary"),
        name="adaln_mod",
    )(cvec, mod_w, mod_b.reshape(depth, 1, n))


def _ffn_kernel(x_ref, vec_ref, w13_ref, w2_ref, o_ref, acc_ref):
    x = x_ref[...]
    h = _modulated(x, vec_ref).astype(BF16)
    for j in range(FFN_HIDDEN // FFN_CHUNK):
        lo = j * FFN_CHUNK
        hi = FFN_HIDDEN + lo
        a = jnp.dot(h, w13_ref[:, lo:lo + FFN_CHUNK].astype(BF16), preferred_element_type=F32)
        b = jnp.dot(h, w13_ref[:, hi:hi + FFN_CHUNK].astype(BF16), preferred_element_type=F32)
        u = (a * _sigmoid(a) * b).astype(BF16)
        y = jnp.dot(u, w2_ref[lo:lo + FFN_CHUNK, :].astype(BF16), preferred_element_type=F32)
        if j == 0:
            acc_ref[...] = y
        else:
            acc_ref[...] += y
    o_ref[...] = x + (0.5 * vec_ref[2:3, :]) * acc_ref[...]


def _ffn_call(x, vec, w13_all, w2_all, layer, which, tm):
    t, d = x.shape
    pick = lambda i: (layer, which, 0, 0)
    return pl.pallas_call(
        _ffn_kernel,
        grid=(t // tm,),
        in_specs=[pl.BlockSpec((tm, d), lambda i: (i, 0)),
                  _full((8, d)),
                  pl.BlockSpec((None, None) + w13_all.shape[2:], pick, pipeline_mode=pl.Buffered(1)),
                  pl.BlockSpec((None, None) + w2_all.shape[2:], pick, pipeline_mode=pl.Buffered(1))],
        out_specs=pl.BlockSpec((tm, d), lambda i: (i, 0)),
        out_shape=jax.ShapeDtypeStruct((t, d), F32),
        scratch_shapes=[pltpu.VMEM((tm, d), F32)],
        compiler_params=_cparams("arbitrary"),
        name="swiglu_half_step",
    )(x, vec, w13_all, w2_all)


def _proj_res_kernel(x_ref, a_ref, vec_ref, w_ref, o_ref):
    y = jnp.dot(a_ref[...], w_ref[...], preferred_element_type=F32)
    o_ref[...] = x_ref[...] + vec_ref[2:3, :] * y


def _proj_res_call(x, a, vec, w, tm):
    t, d = x.shape
    return pl.pallas_call(
        _proj_res_kernel,
        grid=(t // tm,),
        in_specs=[pl.BlockSpec((tm, d), lambda i: (i, 0)),
                  pl.BlockSpec((tm, a.shape[1]), lambda i: (i, 0)),
                  _full((8, d)),
                  _full(w.shape)],
        out_specs=pl.BlockSpec((tm, d), lambda i: (i, 0)),
        out_shape=jax.ShapeDtypeStruct((t, d), F32),
        compiler_params=_cparams("arbitrary"),
        name="out_proj_residual",
    )(x, a, vec, w)


CONV_HALO = 16
CONV_TM = 1024


def _conv_kernel(x_ref, xp_ref, xn_ref, vec_ref, win_ref, k_ref, wout_ref, o_ref, h_ref, acc_ref, *, tm):
    i = pl.program_id(0)
    last = pl.num_programs(0) - 1
    x = x_ref[...]
    d = x.shape[1]
    keep_prev = (i > 0).astype(F32)
    keep_next = (i < last).astype(F32)
    h_ref[0:CONV_HALO, :] = (_modulated(xp_ref[...], vec_ref) * keep_prev).astype(BF16)
    h_ref[CONV_HALO:CONV_HALO + tm, :] = _modulated(x, vec_ref).astype(BF16)
    h_ref[CONV_HALO + tm:, :] = (_modulated(xn_ref[...], vec_ref) * keep_next).astype(BF16)
    rows = tm + 2 * CONV_HALO
    cw = 256
    for j in range(d // cw):
        lo = j * cw
        hall = h_ref[...]
        cg = jnp.dot(hall, win_ref[:, d + lo:d + lo + cw], preferred_element_type=F32)
        xv = jnp.dot(hall, win_ref[:, 2 * d + lo:2 * d + lo + cw], preferred_element_type=F32)
        u = cg * xv
        up = pltpu.roll(u, 1, axis=0)
        un = pltpu.roll(u, rows - 1, axis=0)
        kk = k_ref[:, lo:lo + cw]
        conv = kk[0:1, :] * up + kk[1:2, :] * u + kk[2:3, :] * un
        conv = conv[CONV_HALO:CONV_HALO + tm, :]
        bg = jnp.dot(h_ref[CONV_HALO:CONV_HALO + tm, :], win_ref[:, lo:lo + cw], preferred_element_type=F32)
        y = jnp.dot((bg * conv).astype(BF16), wout_ref[lo:lo + cw, :], preferred_element_type=F32)
        if j == 0:
            acc_ref[...] = y
        else:
            acc_ref[...] += y
    o_ref[...] = x + vec_ref[2:3, :] * acc_ref[...]


def _conv_call(x, vec, w_in_all, k8, w_out_all, j, tm):
    t, d = x.shape
    hb = tm // CONV_HALO
    nhb = t // CONV_HALO
    pick = lambda i: (j, 0, 0)
    return pl.pallas_call(
        functools.partial(_conv_kernel, tm=tm),
        grid=(t // tm,),
        in_specs=[pl.BlockSpec((tm, d), lambda i: (i, 0)),
                  pl.BlockSpec((CONV_HALO, d), lambda i: (jnp.maximum(i * hb - 1, 0), 0)),
                  pl.BlockSpec((CONV_HALO, d), lambda i: (jnp.minimum((i + 1) * hb, nhb - 1), 0)),
                  _full((8, d)),
                  pl.BlockSpec((None,) + w_in_all.shape[1:], pick),
                  _full((8, d)),
                  pl.BlockSpec((None,) + w_out_all.shape[1:], pick)],
        out_specs=pl.BlockSpec((tm, d), lambda i: (i, 0)),
        out_shape=jax.ShapeDtypeStruct((t, d), F32),
        scratch_shapes=[pltpu.VMEM((tm + 2 * CONV_HALO, d), BF16), pltpu.VMEM((tm, d), F32)],
        compiler_params=_cparams("arbitrary"),
        name="short_conv_mixer",
    )(x, x, x, vec, w_in_all, k8, w_out_all)


def _head_rms(t, g):
    return t * lax.rsqrt(jnp.mean(t * t, axis=-1, keepdims=True) + NORM_EPS) * g


def _rope(t, cos, sin):
    lane = lax.broadcasted_iota(jnp.int32, t.shape, 1)
    first = (lane % (HEAD_DIM // 2)) < (HEAD_DIM // 4)
    rot = jnp.where(first, -pltpu.roll(t, HEAD_DIM - HEAD_DIM // 4, axis=1), pltpu.roll(t, HEAD_DIM // 4, axis=1))
    return t * cos + rot * sin


ATT_TQ = 512
ATT_LAG_MAX_EXP = 64.0
ATT_VROWS = HEAD_DIM + 16


def _qkv_kernel(x_ref, vec_ref, wqv_ref, wk_ref, gq_ref, gk_ref, cos_ref, sin_ref, cost_ref, sint_ref,
                q_ref, k_ref, v_ref, *, tq):
    h = _modulated(x_ref[...], vec_ref).astype(BF16)
    tm = h.shape[0]
    pt = lax.dot_general(wqv_ref[...], h, (((1,), (1,)), ((), ())), preferred_element_type=F32)
    pk = jnp.dot(h, wk_ref[...], preferred_element_type=F32)
    cost = cost_ref[...]
    sint = sint_ref[...]
    gq = gq_ref[...]
    qr = HEAD_DIM // 4
    for hq in range(Q_HEADS):
        t = pt[hq * HEAD_DIM:(hq + 1) * HEAD_DIM, :]
        t = t * lax.rsqrt(jnp.mean(t * t, axis=0, keepdims=True) + NORM_EPS) * gq
        rot = jnp.concatenate([-t[qr:2 * qr], t[0:qr], -t[3 * qr:4 * qr], t[2 * qr:3 * qr]], axis=0)
        qt = ((t * cost + rot * sint) * QK_SCALE_LOG2).astype(BF16)
        g = hq % GROUP
        for sub in range(tm // tq):
            q_ref[hq // GROUP, sub, :, g * tq:(g + 1) * tq] = qt[:, sub * tq:(sub + 1) * tq]
    qd = Q_HEADS * HEAD_DIM
    cos = cos_ref[...]
    sin = sin_ref[...]
    for hk in range(KV_HEADS):
        t = _head_rms(pk[:, hk * HEAD_DIM:(hk + 1) * HEAD_DIM], gk_ref[0:1, :])
        k_ref[hk] = _rope(t, cos, sin).astype(BF16)
        v_ref[hk, 0, 0:HEAD_DIM, :] = pt[qd + hk * HEAD_DIM:qd + (hk + 1) * HEAD_DIM, :].astype(BF16)
        v_ref[hk, 0, HEAD_DIM:, :] = jnp.ones((ATT_VROWS - HEAD_DIM, tm), BF16)


def _qkv_call(x, vec, wqv_t, wk, gq_b, gk8, cos, sin, cos_t, sin_t, tm, tq):
    t, d = x.shape
    nsub = tm // tq
    row = lambda i: (i, 0)
    col = lambda i: (0, i)
    return pl.pallas_call(
        functools.partial(_qkv_kernel, tq=tq),
        grid=(t // tm,),
        in_specs=[pl.BlockSpec((tm, d), row),
                  _full((8, d)),
                  _full(wqv_t.shape),
                  _full(wk.shape),
                  _full((HEAD_DIM, tm)),
                  _full((8, HEAD_DIM)),
                  pl.BlockSpec((tm, HEAD_DIM), row),
                  pl.BlockSpec((tm, HEAD_DIM), row),
                  pl.BlockSpec((HEAD_DIM, tm), col),
                  pl.BlockSpec((HEAD_DIM, tm), col)],
        out_specs=[pl.BlockSpec((KV_HEADS, nsub, HEAD_DIM, GROUP * tq), lambda i: (0, i, 0, 0)),
                   pl.BlockSpec((KV_HEADS, tm, HEAD_DIM), lambda i: (0, i, 0)),
                   pl.BlockSpec((KV_HEADS, 1, ATT_VROWS, tm), lambda i: (0, i, 0, 0))],
        out_shape=[jax.ShapeDtypeStruct((KV_HEADS, t // tq, HEAD_DIM, GROUP * tq), BF16),
                   jax.ShapeDtypeStruct((KV_HEADS, t, HEAD_DIM), BF16),
                   jax.ShapeDtypeStruct((KV_HEADS, t // tm, ATT_VROWS, tm), BF16)],
        compiler_params=_cparams("arbitrary"),
        name="gqa_qkv_proj",
    )(x, vec, wqv_t, wk, gq_b, gk8, cos, sin, cos_t, sin_t)


def _flash_kernel(*refs, tk, nk, lagged):
    if nk and not lagged:
        q_ref, kc_ref, vc_ref, k_ref, v_ref, o_ref, m_ref, acc_ref, sa_ref, sb_ref, xa_ref, xb_ref = refs
    elif nk:
        q_ref, kc_ref, vc_ref, k_ref, v_ref, o_ref, m_ref, acc_ref = refs
    else:
        q_ref, kc_ref, vc_ref, o_ref, m_ref, acc_ref = refs
    qt = q_ref[0, 0]

    def scores(ks):
        return jnp.dot(ks, qt, preferred_element_type=F32)

    def key_tile(j):
        return k_ref[0, pl.ds(pl.multiple_of(j * tk, tk), tk), :]

    def produce(j, s_ref, x_ref):
        st = scores(key_tile(j))
        s_ref[...] = st
        x_ref[...] = jnp.max(st, axis=0, keepdims=True)

    def absorb(s_ref, x_ref, vt):
        m_old = m_ref[...]
        m_new = jnp.maximum(m_old, x_ref[...])
        alpha = jnp.exp2(m_old - m_new)
        pt = jnp.exp2(s_ref[...] - m_new).astype(BF16)
        acc_ref[...] = alpha * acc_ref[...] + jnp.dot(vt, pt, preferred_element_type=F32)
        m_ref[...] = m_new

    def absorb_lagged(j):
        st = scores(key_tile(j))
        m_lag = m_ref[...]
        pt = jnp.exp2(st - m_lag).astype(BF16)
        m_new = jnp.maximum(m_lag, jnp.max(st, axis=0, keepdims=True))
        acc = acc_ref[...] + jnp.dot(v_ref[0, j], pt, preferred_element_type=F32)
        acc_ref[...] = acc * jnp.exp2(m_lag - m_new)
        m_ref[...] = m_new

    if nk and not lagged:
        produce(0, sa_ref, xa_ref)

    st = scores(kc_ref[0])
    m0 = jnp.max(st, axis=0, keepdims=True)
    acc_ref[...] = jnp.dot(vc_ref[0, 0], jnp.exp2(st - m0).astype(BF16), preferred_element_type=F32)
    m_ref[...] = m0

    if nk and lagged:
        def body(jj, carry):
            absorb_lagged(2 * jj)
            absorb_lagged(2 * jj + 1)
            return carry

        lax.fori_loop(0, nk // 2, body, 0)
    elif nk:
        def body(jj, carry):
            j = 2 * jj
            produce(j + 1, sb_ref, xb_ref)
            absorb(sa_ref, xa_ref, v_ref[0, j])
            produce(j + 2, sa_ref, xa_ref)
            absorb(sb_ref, xb_ref, v_ref[0, j + 1])
            return carry

        lax.fori_loop(0, nk // 2 - 1, body, 0)
        produce(nk - 1, sb_ref, xb_ref)
        absorb(sa_ref, xa_ref, v_ref[0, nk - 2])
        absorb(sb_ref, xb_ref, v_ref[0, nk - 1])

    acc = acc_ref[...]
    out = acc[0:HEAD_DIM, :] / acc[HEAD_DIM:HEAD_DIM + 1, :]
    tq = o_ref.shape[0]
    for g in range(GROUP):
        o_ref[:, g * HEAD_DIM:(g + 1) * HEAD_DIM] = jnp.transpose(out[:, g * tq:(g + 1) * tq]).astype(o_ref.dtype)


def _flash_call(q, kc, vc, k, v, lagged=False):
    nq, cols = q.shape[1], q.shape[3]
    tq = cols // GROUP
    head3 = lambda h, i: (h, 0, 0)
    head4 = lambda h, i: (h, 0, 0, 0)
    in_specs = [pl.BlockSpec((1, 1, HEAD_DIM, cols), lambda h, i: (h, i, 0, 0)),
                pl.BlockSpec((1,) + kc.shape[1:], head3), pl.BlockSpec((1,) + vc.shape[1:], head4)]
    args = [q, kc, vc]
    scratch = [pltpu.VMEM((1, cols), F32), pltpu.VMEM((ATT_VROWS, cols), F32)]
    nk = tk = 0
    if k is not None:
        nk, tk = v.shape[1], v.shape[3]
        in_specs += [pl.BlockSpec((1,) + k.shape[1:], head3), pl.BlockSpec((1,) + v.shape[1:], head4)]
        args += [k, v]
        assert nk % 2 == 0 and nk >= 2
        if not lagged:
            scratch += [pltpu.VMEM((tk, cols), F32), pltpu.VMEM((tk, cols), F32),
                        pltpu.VMEM((1, cols), F32), pltpu.VMEM((1, cols), F32)]
    return pl.pallas_call(
        functools.partial(_flash_kernel, tk=tk, nk=nk, lagged=lagged),
        grid=(KV_HEADS, nq),
        in_specs=in_specs,
        out_specs=pl.BlockSpec((tq, GROUP * HEAD_DIM), lambda h, i: (i, h)),
        out_shape=jax.ShapeDtypeStruct((nq * tq, Q_HEADS * HEAD_DIM), BF16),
        scratch_shapes=scratch,
        compiler_params=_cparams("arbitrary", "arbitrary"),
        name="gqa_flash_lagged" if lagged else "gqa_flash",
    )(*args)


def _ml_proj_kernel(x_ref, vec_ref, wq_ref, wkt_ref, wv_ref, wg_ref, wgt_ref, bg_ref, bgt_ref,
                    q_ref, kt_ref, v_ref, gc_ref, gr_ref):
    h = _modulated(x_ref[...], vec_ref).astype(BF16)
    nt = (((1,), (1,)), ((), ()))
    q_ref[...] = jnp.dot(h, wq_ref[...], preferred_element_type=F32).astype(BF16)
    kt = lax.dot_general(wkt_ref[...], h, nt, preferred_element_type=F32)
    kt_ref[...] = (kt * (ML_DK ** -0.5)).astype(BF16)
    v_ref[...] = jnp.dot(h, wv_ref[...], preferred_element_type=F32).astype(BF16)
    gc_ref[...] = jnp.dot(h, wg_ref[...], preferred_element_type=F32) + bg_ref[...]
    gr_ref[...] = lax.dot_general(wgt_ref[...], h, nt, preferred_element_type=F32) + bgt_ref[...]


def _ml_proj_call(x, vec, wq, wkt, wv, wg, wgt, bg, bgt, tm):
    t, d = x.shape
    row = lambda i: (i, 0)
    col = lambda i: (0, i)
    return pl.pallas_call(
        _ml_proj_kernel,
        grid=(t // tm,),
        in_specs=[pl.BlockSpec((tm, d), row), _full((8, d)), _full(wq.shape), _full(wkt.shape), _full(wv.shape),
                  _full(wg.shape), _full(wgt.shape), _full(bg.shape), _full(bgt.shape)],
        out_specs=[pl.BlockSpec((tm, ML_KD), row), pl.BlockSpec((ML_KD, tm), col), pl.BlockSpec((tm, d), row),
                   pl.BlockSpec((tm, ML_GATES), row), pl.BlockSpec((ML_GATES, tm), col)],
        out_shape=[jax.ShapeDtypeStruct((t, ML_KD), BF16), jax.ShapeDtypeStruct((ML_KD, t), BF16),
                   jax.ShapeDtypeStruct((t, d), BF16), jax.ShapeDtypeStruct((t, ML_GATES), F32),
                   jax.ShapeDtypeStruct((ML_GATES, t), F32)],
        compiler_params=_cparams("arbitrary"),
        name="mlstm_in_proj",
    )(x, vec, wq, wkt, wv, wg, wgt, bg, bgt)


def _ml_scan_kernel(qf_ref, ktf_ref, vf_ref, gcf_ref, grf_ref, qb_ref, ktb_ref, vb_ref, gcb_ref, grb_ref,
                    c0_ref, m0_ref, hf_ref, hb_ref, cout_ref, mout_ref, c_ref, m_ref):
    n = pl.program_id(0)
    L = ML_CHUNK

    @pl.when(n == 0)
    def _():
        c_ref[...] = c0_ref[...]
        m_ref[...] = m0_ref[...]

    r = lax.broadcasted_iota(jnp.int32, (L, L), 0)
    s_ = lax.broadcasted_iota(jnp.int32, (L, L), 1)
    lower = r >= s_
    upper = r <= s_
    ones_blk = jnp.ones((L, ML_STATE_W - ML_DV), BF16)
    hi = lax.Precision.HIGHEST
    per_dir = ((qf_ref, ktf_ref, vf_ref, gcf_ref, grf_ref, hf_ref, lower, upper, L - 1),
               (qb_ref, ktb_ref, vb_ref, gcb_ref, grb_ref, hb_ref, upper, lower, 0))
    for d, (q_ref, kt_ref, v_ref, gc_ref, gr_ref, h_ref, seen, seen_t, end) in enumerate(per_dir):
        gcol = gc_ref[...]
        grow = gr_ref[...]
        a_col = jnp.dot(seen.astype(F32), _log_sigmoid(gcol), precision=hi, preferred_element_type=F32)
        a_row = jnp.dot(_log_sigmoid(grow), seen_t.astype(F32), precision=hi, preferred_element_type=F32)
        for hd in range(ML_HEADS):
            ji = (2 * d) * ML_HEADS + hd
            jf = (2 * d + 1) * ML_HEADS + hd
            st = d * ML_HEADS + hd
            a_c = a_col[:, jf:jf + 1]
            a_r = a_row[jf:jf + 1, :]
            b_r = grow[ji:ji + 1, :] - a_r
            a_end = a_r[:, end:end + 1]
            m_prev = m_ref[st:st + 1, 0:1]
            dm = jnp.where(seen, a_c + b_r, -jnp.inf)
            inter = a_c + m_prev
            m_t = jnp.maximum(jnp.max(dm, axis=1, keepdims=True), inter)
            w = jnp.exp(dm - m_t)
            qh = q_ref[:, hd * ML_DK:(hd + 1) * ML_DK]
            kth = kt_ref[hd * ML_DK:(hd + 1) * ML_DK, :]
            sw = jnp.dot(qh, kth, preferred_element_type=F32) * w
            sc = jnp.exp(inter - m_t)
            vp = jnp.concatenate([v_ref[:, hd * ML_DV:(hd + 1) * ML_DV], ones_blk], axis=1)
            cst = c_ref[st]
            hx = (jnp.dot(sw.astype(BF16), vp, preferred_element_type=F32)
                  + sc * jnp.dot(qh, cst.astype(BF16), preferred_element_type=F32))
            den = jnp.maximum(jnp.abs(hx[:, ML_DV:]), jnp.exp(-m_t))
            inv = 1.0 / den
            h_ref[:, hd * ML_DV:hd * ML_DV + 128] = hx[:, 0:128] * inv
            h_ref[:, hd * ML_DV + 128:(hd + 1) * ML_DV] = hx[:, 128:ML_DV] * inv
            w_end = a_end + b_r
            m_new = jnp.maximum(a_end + m_prev, jnp.max(w_end, axis=1, keepdims=True))
            e_r = jnp.exp(w_end - m_new)
            decay = jnp.exp(a_end + m_prev - m_new)
            ke = (kth.astype(F32) * e_r).astype(BF16)
            c_ref[st] = decay * cst + jnp.dot(ke, vp, preferred_element_type=F32)
            m_ref[st:st + 1, :] = jnp.broadcast_to(m_new, (1, 128))

    @pl.when(n == pl.num_programs(0) - 1)
    def _():
        cout_ref[...] = c_ref[...]
        mout_ref[...] = m_ref[...]


def _ml_scan_call(q, kt, v, gc, gr, c0, m0):
    t, d = v.shape
    L = ML_CHUNK
    nc = t // L
    fr = lambda n: (n, 0)
    fc = lambda n: (0, n)
    br = lambda n: (nc - 1 - n, 0)
    bc = lambda n: (0, nc - 1 - n)

    def specs(rm, cm):
        return [pl.BlockSpec((L, ML_KD), rm), pl.BlockSpec((ML_KD, L), cm), pl.BlockSpec((L, d), rm),
                pl.BlockSpec((L, ML_GATES), rm), pl.BlockSpec((ML_GATES, L), cm)]

    nst = 2 * ML_HEADS
    return pl.pallas_call(
        _ml_scan_kernel,
        grid=(nc,),
        in_specs=specs(fr, fc) + specs(br, bc) + [_full(c0.shape), _full(m0.shape)],
        out_specs=[pl.BlockSpec((L, d), fr), pl.BlockSpec((L, d), br), _full(c0.shape), _full(m0.shape)],
        out_shape=[jax.ShapeDtypeStruct((t, d), F32), jax.ShapeDtypeStruct((t, d), F32),
                   jax.ShapeDtypeStruct(c0.shape, F32), jax.ShapeDtypeStruct(m0.shape, F32)],
        scratch_shapes=[pltpu.VMEM((nst, ML_DK, ML_STATE_W), F32), pltpu.VMEM((nst, 128), F32)],
        compiler_params=_cparams("arbitrary"),
        name="mlstm_scan",
    )(q, kt, v, gc, gr, q, kt, v, gc, gr, c0, m0)


def _ml_out_kernel(x_ref, hf_ref, hb_ref, vec_ref, ng_ref, woi_ref, wo_ref, o_ref):
    x = x_ref[...]
    h = _modulated(x, vec_ref).astype(BF16)
    o = jnp.dot(h, woi_ref[...], preferred_element_type=F32)
    hs = hf_ref[...] + hb_ref[...]
    parts = []
    for hd in range(ML_HEADS):
        seg = hs[:, hd * ML_DV:(hd + 1) * ML_DV]
        parts.append(_head_rms(seg, ng_ref[0:1, hd * ML_DV:(hd + 1) * ML_DV]))
    hn = jnp.concatenate(parts, axis=1)
    y = jnp.dot((_sigmoid(o) * hn).astype(BF16), wo_ref[...], preferred_element_type=F32)
    o_ref[...] = x + vec_ref[2:3, :] * y


def _ml_out_call(x, hf, hb, vec, ng8, woi, wo, tm):
    t, d = x.shape
    row = lambda i: (i, 0)
    return pl.pallas_call(
        _ml_out_kernel,
        grid=(t // tm,),
        in_specs=[pl.BlockSpec((tm, d), row), pl.BlockSpec((tm, d), row), pl.BlockSpec((tm, d), row),
                  _full((8, d)), _full((8, d)), _full(woi.shape), _full(wo.shape)],
        out_specs=pl.BlockSpec((tm, d), row),
        out_shape=jax.ShapeDtypeStruct((t, d), F32),
        compiler_params=_cparams("arbitrary"),
        name="mlstm_out",
    )(x, hf, hb, vec, ng8, woi, wo)


def _pad8(rows):
    a = jnp.stack(rows)
    return jnp.concatenate([a, jnp.zeros((8 - a.shape[0], a.shape[1]), a.dtype)], axis=0)


def _rope_tables(n_tok):
    rows = n_tok // GRID_W
    seg = HEAD_DIM // 2
    inv = ROPE_THETA ** (-jnp.arange(seg // 2, dtype=F32) / (seg // 2))
    ang_r = jnp.arange(rows).astype(F32)[:, None] * inv
    ang_c = jnp.arange(GRID_W).astype(F32)[:, None] * inv

    def expand(fr, fc):
        r = jnp.repeat(fr, GRID_W, axis=0)
        c = jnp.tile(fc, (rows, 1))
        return jnp.concatenate([r, r, c, c], axis=-1)

    return expand(jnp.cos(ang_r), jnp.cos(ang_c)), expand(jnp.sin(ang_r), jnp.sin(ang_c))


def _ctx_read_at_or_after(i):
    return any((j % N_MIXERS) != 0 for j in range(i, DEPTH))


def kernel(x, c, ctx, c_ctx, mod_w, mod_b, norm_g, ffn_w13, ffn_w2, conv_w_in, conv_k, conv_w_out,
           attn_w_qkv, attn_q_g, attn_k_g, attn_w_o, mlstm_w_in, mlstm_b_gate, mlstm_norm_g, mlstm_w_o):
    d = D_MODEL
    t_lat = x.shape[1]
    t_ctx = ctx.shape[1]
    tm = 512
    tmc = t_ctx
    lat = x[0]
    cx = ctx[0]

    cvec = _pad8([c[0], c_ctx])
    modall = _mod_call(cvec, mod_w, mod_b)

    m5 = jnp.transpose(modall[:, :2, :].reshape(DEPTH, 2, 3, 3, d), (0, 2, 1, 3, 4))
    g5 = jnp.broadcast_to(norm_g[:, :, None, None, :], (DEPTH, 3, 2, 1, d))
    vecs_all = jnp.concatenate([m5, g5, jnp.zeros((DEPTH, 3, 2, 4, d), F32)], axis=3)

    def vec(i, a, stream):
        return vecs_all[i, a, stream]

    cos, sin = _rope_tables(t_lat)
    w13_all, w2_all = ffn_w13, ffn_w2
    conv_in_all, conv_out_all = conv_w_in.astype(BF16), conv_w_out.astype(BF16)
    counters = [0, 0, 0]
    for i in range(DEPTH):
        kind = i % N_MIXERS
        j = counters[kind]
        counters[kind] += 1
        ctx_out = _ctx_read_at_or_after(i + 1)
        if not _ctx_read_at_or_after(i):
            cx = None
        lat = _ffn_call(lat, vec(i, 0, 0), w13_all, w2_all, i, 0, tm)
        if cx is not None:
            cx = _ffn_call(cx, vec(i, 0, 1), w13_all, w2_all, i, 0, tmc)
        vl, vc = vec(i, 1, 0), vec(i, 1, 1)
        if kind == 0:
            k8 = _pad8([conv_k[j, 0], conv_k[j, 1], conv_k[j, 2]])
            lat = _conv_call(lat, vl, conv_in_all, k8, conv_out_all, j, CONV_TM)
            if ctx_out:
                cx = _conv_call(cx, vc, conv_in_all, k8, conv_out_all, j, tmc)
        elif kind == 1:
            w_qkv, w_o = attn_w_qkv[j], attn_w_o[j].astype(BF16)
            qd, kd = Q_HEADS * HEAD_DIM, KV_HEADS * HEAD_DIM
            wqv_t = jnp.concatenate([w_qkv[:, :qd], w_qkv[:, qd + kd:]], axis=1).T.astype(BF16)
            wk = w_qkv[:, qd:qd + kd].astype(BF16)
            gk8 = _pad8([attn_k_g[j]])
            gq = attn_q_g[j][:, None]
            one, zero = jnp.ones((t_ctx, HEAD_DIM), F32), jnp.zeros((t_ctx, HEAD_DIM), F32)
            q, k, v = _qkv_call(lat, vl, wqv_t, wk, jnp.broadcast_to(gq, (HEAD_DIM, tm)), gk8,
                                cos, sin, cos.T, sin.T, tm, ATT_TQ)
            qc, kc, vc_ = _qkv_call(cx, vc, wqv_t, wk, jnp.broadcast_to(gq, (HEAD_DIM, tmc)), gk8,
                                    one, zero, one.T, zero.T, tmc, t_ctx)
            bound = (HEAD_DIM * QK_SCALE_LOG2) * jnp.max(jnp.abs(attn_q_g[j])) * jnp.max(jnp.abs(attn_k_g[j]))
            att = lax.cond(2.0 * bound <= ATT_LAG_MAX_EXP,
                           lambda *a: _flash_call(*a, lagged=True), lambda *a: _flash_call(*a), q, kc, vc_, k, v)
            lat = _proj_res_call(lat, att, vl, w_o, tm)
            if ctx_out:
                attc = _flash_call(qc, kc, vc_, None, None)
                cx = _proj_res_call(cx, attc, vc, w_o, tmc)
        else:
            w_in = mlstm_w_in[j]
            kd = ML_KD
            g0 = kd + d
            q0 = g0 + ML_GATES
            o0 = q0 + kd
            wkt = w_in[:, :kd].T.astype(BF16)
            wv = w_in[:, kd:g0].astype(BF16)
            wg = w_in[:, g0:q0].astype(BF16)
            wq = w_in[:, q0:o0].astype(BF16)
            woi = w_in[:, o0:].astype(BF16)
            bg = mlstm_b_gate[j].reshape(1, ML_GATES)
            pc = _ml_proj_call(cx, vc, wq, wkt, wv, wg, wg.T, bg, bg.T, tmc)
            pl_ = _ml_proj_call(lat, vl, wq, wkt, wv, wg, wg.T, bg, bg.T, tm)
            c0 = jnp.zeros((2 * ML_HEADS, ML_DK, ML_STATE_W), F32)
            m0 = jnp.zeros((2 * ML_HEADS, 128), F32)
            hfc, hbc, c1, m1 = _ml_scan_call(*pc, c0, m0)
            hf, hb, _, _ = _ml_scan_call(*pl_, c1, m1)
            ng8 = _pad8([mlstm_norm_g[j]])
            wo = mlstm_w_o[j].astype(BF16)
            lat = _ml_out_call(lat, hf, hb, vl, ng8, woi, wo, tm)
            if ctx_out:
                cx = _ml_out_call(cx, hfc, hbc, vc, ng8, woi, wo, tmc)
        if ctx_out:
            cx = _ffn_call(cx, vec(i, 2, 1), w13_all, w2_all, i, 1, tmc)
        else:
            cx = None
        lat = _ffn_call(lat, vec(i, 2, 0), w13_all, w2_all, i, 1, tm)
    return lat[None]
```

```python
import functools

import jax
import jax.numpy as jnp
from jax import lax
from jax.experimental import pallas as pl
from jax.experimental.pallas import tpu as pltpu

F32 = jnp.float32
BF16 = jnp.bfloat16

D_MODEL = 1024
DEPTH = 4
N_MIXERS = 3
NORM_EPS = 1e-6
FFN_HIDDEN = 2816
FFN_CHUNK = 256
GRID_W = 64
HEAD_DIM = 128
Q_HEADS = 8
KV_HEADS = 2
GROUP = Q_HEADS // KV_HEADS
ROPE_THETA = 10000.0
ML_HEADS = 4
ML_DV = 256
ML_DK = 128
ML_KD = ML_HEADS * ML_DK
ML_GATES = 4 * ML_HEADS
ML_CHUNK = 256
ML_STATE_W = ML_DV + 128
QK_SCALE_LOG2 = (HEAD_DIM ** -0.5) * 1.4426950408889634

VMEM_LIMIT = 56 * 1024 * 1024


def _cparams(*sem):
    return pltpu.CompilerParams(dimension_semantics=sem, vmem_limit_bytes=VMEM_LIMIT)


def _full(shape):
    return pl.BlockSpec(shape, lambda *_: (0,) * len(shape))


def _modulated(x, vec_ref):
    ms = jnp.mean(x * x, axis=-1, keepdims=True)
    y = x * lax.rsqrt(ms + NORM_EPS) * vec_ref[3:4, :]
    return y * (1.0 + vec_ref[1:2, :]) + vec_ref[0:1, :]


def _sigmoid(x):
    return 1.0 / (1.0 + jnp.exp(-x))


def _log_sigmoid(x):
    return jnp.minimum(x, 0.0) - jnp.log(1.0 + jnp.exp(-jnp.abs(x)))


def _mod_kernel(c_ref, w_ref, b_ref, o_ref):
    c = c_ref[...]
    s = (c * _sigmoid(c)).astype(BF16)
    o_ref[0] = jnp.dot(s, w_ref[0].astype(BF16), preferred_element_type=F32) + b_ref[0]


def _mod_call(cvec, mod_w, mod_b):
    depth, d, n = mod_w.shape
    tn = 1024
    return pl.pallas_call(
        _mod_kernel,
        grid=(depth, n // tn),
        in_specs=[_full((8, d)),
                  pl.BlockSpec((1, d, tn), lambda l, j: (l, 0, j)),
                  pl.BlockSpec((1, 1, tn), lambda l, j: (l, 0, j))],
        out_specs=pl.BlockSpec((1, 8, tn), lambda l, j: (l, 0, j)),
        out_shape=jax.ShapeDtypeStruct((depth, 8, n), F32),
        compiler_params=_cparams("arbitrary", "arbitrary"),
        name="adaln_mod",
    )(cvec, mod_w, mod_b.reshape(depth, 1, n))


def _ffn_kernel(x_ref, vec_ref, w13_ref, w2_ref, o_ref, acc_ref):
    x = x_ref[...]
    h = _modulated(x, vec_ref).astype(BF16)
    for j in range(FFN_HIDDEN // FFN_CHUNK):
        lo = j * FFN_CHUNK
        hi = FFN_HIDDEN + lo
        a = jnp.dot(h, w13_ref[:, lo:lo + FFN_CHUNK].astype(BF16), preferred_element_type=F32)
        b = jnp.dot(h, w13_ref[:, hi:hi + FFN_CHUNK].astype(BF16), preferred_element_type=F32)
        u = (a * _sigmoid(a) * b).astype(BF16)
        y = jnp.dot(u, w2_ref[lo:lo + FFN_CHUNK, :].astype(BF16), preferred_element_type=F32)
        if j == 0:
            acc_ref[...] = y
        else:
            acc_ref[...] += y
    o_ref[...] = x + (0.5 * vec_ref[2:3, :]) * acc_ref[...]


def _ffn_call(x, vec, w13_all, w2_all, layer, which, tm):
    t, d = x.shape
    pick = lambda i: (layer, which, 0, 0)
    return pl.pallas_call(
        _ffn_kernel,
        grid=(t // tm,),
        in_specs=[pl.BlockSpec((tm, d), lambda i: (i, 0)),
                  _full((8, d)),
                  pl.BlockSpec((None, None) + w13_all.shape[2:], pick, pipeline_mode=pl.Buffered(1)),
                  pl.BlockSpec((None, None) + w2_all.shape[2:], pick, pipeline_mode=pl.Buffered(1))],
        out_specs=pl.BlockSpec((tm, d), lambda i: (i, 0)),
        out_shape=jax.ShapeDtypeStruct((t, d), F32),
        scratch_shapes=[pltpu.VMEM((tm, d), F32)],
        compiler_params=_cparams("arbitrary"),
        name="swiglu_half_step",
    )(x, vec, w13_all, w2_all)


def _proj_res_kernel(x_ref, a_ref, vec_ref, w_ref, o_ref):
    y = jnp.dot(a_ref[...], w_ref[...], preferred_element_type=F32)
    o_ref[...] = x_ref[...] + vec_ref[2:3, :] * y


def _proj_res_call(x, a, vec, w, tm):
    t, d = x.shape
    return pl.pallas_call(
        _proj_res_kernel,
        grid=(t // tm,),
        in_specs=[pl.BlockSpec((tm, d), lambda i: (i, 0)),
                  pl.BlockSpec((tm, a.shape[1]), lambda i: (i, 0)),
                  _full((8, d)),
                  _full(w.shape)],
        out_specs=pl.BlockSpec((tm, d), lambda i: (i, 0)),
        out_shape=jax.ShapeDtypeStruct((t, d), F32),
        compiler_params=_cparams("arbitrary"),
        name="out_proj_residual",
    )(x, a, vec, w)


CONV_HALO = 16
CONV_TM = 1024


def _conv_kernel(x_ref, xp_ref, xn_ref, vec_ref, win_ref, k_ref, wout_ref, o_ref, h_ref, acc_ref, *, tm):
    i = pl.program_id(0)
    last = pl.num_programs(0) - 1
    x = x_ref[...]
    d = x.shape[1]
    keep_prev = (i > 0).astype(F32)
    keep_next = (i < last).astype(F32)
    h_ref[0:CONV_HALO, :] = (_modulated(xp_ref[...], vec_ref) * keep_prev).astype(BF16)
    h_ref[CONV_HALO:CONV_HALO + tm, :] = _modulated(x, vec_ref).astype(BF16)
    h_ref[CONV_HALO + tm:, :] = (_modulated(xn_ref[...], vec_ref) * keep_next).astype(BF16)
    rows = tm + 2 * CONV_HALO
    cw = 256
    for j in range(d // cw):
        lo = j * cw
        hall = h_ref[...]
        cg = jnp.dot(hall, win_ref[:, d + lo:d + lo + cw], preferred_element_type=F32)
        xv = jnp.dot(hall, win_ref[:, 2 * d + lo:2 * d + lo + cw], preferred_element_type=F32)
        u = cg * xv
        up = pltpu.roll(u, 1, axis=0)
        un = pltpu.roll(u, rows - 1, axis=0)
        kk = k_ref[:, lo:lo + cw]
        conv = kk[0:1, :] * up + kk[1:2, :] * u + kk[2:3, :] * un
        conv = conv[CONV_HALO:CONV_HALO + tm, :]
        bg = jnp.dot(h_ref[CONV_HALO:CONV_HALO + tm, :], win_ref[:, lo:lo + cw], preferred_element_type=F32)
        y = jnp.dot((bg * conv).astype(BF16), wout_ref[lo:lo + cw, :], preferred_element_type=F32)
        if j == 0:
            acc_ref[...] = y
        else:
            acc_ref[...] += y
    o_ref[...] = x + vec_ref[2:3, :] * acc_ref[...]


def _conv_call(x, vec, w_in_all, k8, w_out_all, j, tm):
    t, d = x.shape
    hb = tm // CONV_HALO
    nhb = t // CONV_HALO
    pick = lambda i: (j, 0, 0)
    return pl.pallas_call(
        functools.partial(_conv_kernel, tm=tm),
        grid=(t // tm,),
        in_specs=[pl.BlockSpec((tm, d), lambda i: (i, 0)),
                  pl.BlockSpec((CONV_HALO, d), lambda i: (jnp.maximum(i * hb - 1, 0), 0)),
                  pl.BlockSpec((CONV_HALO, d), lambda i: (jnp.minimum((i + 1) * hb, nhb - 1), 0)),
                  _full((8, d)),
                  pl.BlockSpec((None,) + w_in_all.shape[1:], pick),
                  _full((8, d)),
                  pl.BlockSpec((None,) + w_out_all.shape[1:], pick)],
        out_specs=pl.BlockSpec((tm, d), lambda i: (i, 0)),
        out_shape=jax.ShapeDtypeStruct((t, d), F32),
        scratch_shapes=[pltpu.VMEM((tm + 2 * CONV_HALO, d), BF16), pltpu.VMEM((tm, d), F32)],
        compiler_params=_cparams("arbitrary"),
        name="short_conv_mixer",
    )(x, x, x, vec, w_in_all, k8, w_out_all)


def _head_rms(t, g):
    return t * lax.rsqrt(jnp.mean(t * t, axis=-1, keepdims=True) + NORM_EPS) * g


def _rope(t, cos, sin):
    lane = lax.broadcasted_iota(jnp.int32, t.shape, 1)
    first = (lane % (HEAD_DIM // 2)) < (HEAD_DIM // 4)
    rot = jnp.where(first, -pltpu.roll(t, HEAD_DIM - HEAD_DIM // 4, axis=1), pltpu.roll(t, HEAD_DIM // 4, axis=1))
    return t * cos + rot * sin


ATT_TQ = 512
ATT_LAG_MAX_EXP = 64.0
ATT_VROWS = HEAD_DIM + 16


def _qkv_kernel(x_ref, vec_ref, wqv_ref, wk_ref, gq_ref, gk_ref, cos_ref, sin_ref, cost_ref, sint_ref,
                q_ref, k_ref, v_ref, *, tq):
    h = _modulated(x_ref[...], vec_ref).astype(BF16)
    tm = h.shape[0]
    pt = lax.dot_general(wqv_ref[...], h, (((1,), (1,)), ((), ())), preferred_element_type=F32)
    pk = jnp.dot(h, wk_ref[...], preferred_element_type=F32)
    cost = cost_ref[...]
    sint = sint_ref[...]
    gq = gq_ref[...]
    qr = HEAD_DIM // 4
    for hq in range(Q_HEADS):
        t = pt[hq * HEAD_DIM:(hq + 1) * HEAD_DIM, :]
        t = t * lax.rsqrt(jnp.mean(t * t, axis=0, keepdims=True) + NORM_EPS) * gq
        rot = jnp.concatenate([-t[qr:2 * qr], t[0:qr], -t[3 * qr:4 * qr], t[2 * qr:3 * qr]], axis=0)
        qt = ((t * cost + rot * sint) * QK_SCALE_LOG2).astype(BF16)
        g = hq % GROUP
        for sub in range(tm // tq):
            q_ref[hq // GROUP, sub, :, g * tq:(g + 1) * tq] = qt[:, sub * tq:(sub + 1) * tq]
    qd = Q_HEADS * HEAD_DIM
    cos = cos_ref[...]
    sin = sin_ref[...]
    for hk in range(KV_HEADS):
        t = _head_rms(pk[:, hk * HEAD_DIM:(hk + 1) * HEAD_DIM], gk_ref[0:1, :])
        k_ref[hk] = _rope(t, cos, sin).astype(BF16)
        v_ref[hk, 0, 0:HEAD_DIM, :] = pt[qd + hk * HEAD_DIM:qd + (hk + 1) * HEAD_DIM, :].astype(BF16)
        v_ref[hk, 0, HEAD_DIM:, :] = jnp.ones((ATT_VROWS - HEAD_DIM, tm), BF16)


def _qkv_call(x, vec, wqv_t, wk, gq_b, gk8, cos, sin, cos_t, sin_t, tm, tq):
    t, d = x.shape
    nsub = tm // tq
    row = lambda i: (i, 0)
    col = lambda i: (0, i)
    return pl.pallas_call(
        functools.partial(_qkv_kernel, tq=tq),
        grid=(t // tm,),
        in_specs=[pl.BlockSpec((tm, d), row),
                  _full((8, d)),
                  _full(wqv_t.shape),
                  _full(wk.shape),
                  _full((HEAD_DIM, tm)),
                  _full((8, HEAD_DIM)),
                  pl.BlockSpec((tm, HEAD_DIM), row),
                  pl.BlockSpec((tm, HEAD_DIM), row),
                  pl.BlockSpec((HEAD_DIM, tm), col),
                  pl.BlockSpec((HEAD_DIM, tm), col)],
        out_specs=[pl.BlockSpec((KV_HEADS, nsub, HEAD_DIM, GROUP * tq), lambda i: (0, i, 0, 0)),
                   pl.BlockSpec((KV_HEADS, tm, HEAD_DIM), lambda i: (0, i, 0)),
                   pl.BlockSpec((KV_HEADS, 1, ATT_VROWS, tm), lambda i: (0, i, 0, 0))],
        out_shape=[jax.ShapeDtypeStruct((KV_HEADS, t // tq, HEAD_DIM, GROUP * tq), BF16),
                   jax.ShapeDtypeStruct((KV_HEADS, t, HEAD_DIM), BF16),
                   jax.ShapeDtypeStruct((KV_HEADS, t // tm, ATT_VROWS, tm), BF16)],
        compiler_params=_cparams("arbitrary"),
        name="gqa_qkv_proj",
    )(x, vec, wqv_t, wk, gq_b, gk8, cos, sin, cos_t, sin_t)


def _flash_kernel(*refs, tk, nk, lagged):
    if nk and not lagged:
        q_ref, kc_ref, vc_ref, k_ref, v_ref, o_ref, m_ref, acc_ref, sa_ref, sb_ref, xa_ref, xb_ref = refs
    elif nk:
        q_ref, kc_ref, vc_ref, k_ref, v_ref, o_ref, m_ref, acc_ref = refs
    else:
        q_ref, kc_ref, vc_ref, o_ref, m_ref, acc_ref = refs
    qt = q_ref[0, 0]

    def scores(ks):
        return jnp.dot(ks, qt, preferred_element_type=F32)

    def key_tile(j):
        return k_ref[0, pl.ds(pl.multiple_of(j * tk, tk), tk), :]

    def produce(j, s_ref, x_ref):
        st = scores(key_tile(j))
        s_ref[...] = st
        x_ref[...] = jnp.max(st, axis=0, keepdims=True)

    def absorb(s_ref, x_ref, vt):
        m_old = m_ref[...]
        m_new = jnp.maximum(m_old, x_ref[...])
        alpha = jnp.exp2(m_old - m_new)
        pt = jnp.exp2(s_ref[...] - m_new).astype(BF16)
        acc_ref[...] = alpha * acc_ref[...] + jnp.dot(vt, pt, preferred_element_type=F32)
        m_ref[...] = m_new

    def absorb_lagged(j):
        st = scores(key_tile(j))
        m_lag = m_ref[...]
        pt = jnp.exp2(st - m_lag).astype(BF16)
        m_new = jnp.maximum(m_lag, jnp.max(st, axis=0, keepdims=True))
        acc = acc_ref[...] + jnp.dot(v_ref[0, j], pt, preferred_element_type=F32)
        acc_ref[...] = acc * jnp.exp2(m_lag - m_new)
        m_ref[...] = m_new

    if nk and not lagged:
        produce(0, sa_ref, xa_ref)

    st = scores(kc_ref[0])
    m0 = jnp.max(st, axis=0, keepdims=True)
    acc_ref[...] = jnp.dot(vc_ref[0, 0], jnp.exp2(st - m0).astype(BF16), preferred_element_type=F32)
    m_ref[...] = m0

    if nk and lagged:
        group = 4 if nk % 4 == 0 else 2

        def body(jj, carry):
            for u in range(group):
                absorb_lagged(group * jj + u)
            return carry

        lax.fori_loop(0, nk // group, body, 0)
    elif nk:
        def body(jj, carry):
            j = 2 * jj
            produce(j + 1, sb_ref, xb_ref)
            absorb(sa_ref, xa_ref, v_ref[0, j])
            produce(j + 2, sa_ref, xa_ref)
            absorb(sb_ref, xb_ref, v_ref[0, j + 1])
            return carry

        lax.fori_loop(0, nk // 2 - 1, body, 0)
        produce(nk - 1, sb_ref, xb_ref)
        absorb(sa_ref, xa_ref, v_ref[0, nk - 2])
        absorb(sb_ref, xb_ref, v_ref[0, nk - 1])

    acc = acc_ref[...]
    out = acc[0:HEAD_DIM, :] / acc[HEAD_DIM:HEAD_DIM + 1, :]
    tq = o_ref.shape[0]
    for g in range(GROUP):
        o_ref[:, g * HEAD_DIM:(g + 1) * HEAD_DIM] = jnp.transpose(out[:, g * tq:(g + 1) * tq]).astype(o_ref.dtype)


def _flash_call(q, kc, vc, k, v, lagged=False):
    nq, cols = q.shape[1], q.shape[3]
    tq = cols // GROUP
    head3 = lambda h, i: (h, 0, 0)
    head4 = lambda h, i: (h, 0, 0, 0)
    in_specs = [pl.BlockSpec((1, 1, HEAD_DIM, cols), lambda h, i: (h, i, 0, 0)),
                pl.BlockSpec((1,) + kc.shape[1:], head3), pl.BlockSpec((1,) + vc.shape[1:], head4)]
    args = [q, kc, vc]
    scratch = [pltpu.VMEM((1, cols), F32), pltpu.VMEM((ATT_VROWS, cols), F32)]
    nk = tk = 0
    if k is not None:
        nk, tk = v.shape[1], v.shape[3]
        in_specs += [pl.BlockSpec((1,) + k.shape[1:], head3), pl.BlockSpec((1,) + v.shape[1:], head4)]
        args += [k, v]
        assert nk % 2 == 0 and nk >= 2
        if not lagged:
            scratch += [pltpu.VMEM((tk, cols), F32), pltpu.VMEM((tk, cols), F32),
                        pltpu.VMEM((1, cols), F32), pltpu.VMEM((1, cols), F32)]
    return pl.pallas_call(
        functools.partial(_flash_kernel, tk=tk, nk=nk, lagged=lagged),
        grid=(KV_HEADS, nq),
        in_specs=in_specs,
        out_specs=pl.BlockSpec((tq, GROUP * HEAD_DIM), lambda h, i: (i, h)),
        out_shape=jax.ShapeDtypeStruct((nq * tq, Q_HEADS * HEAD_DIM), BF16),
        scratch_shapes=scratch,
        compiler_params=_cparams("arbitrary", "arbitrary"),
        name="gqa_flash_lagged" if lagged else "gqa_flash",
    )(*args)


def _ml_proj_kernel(x_ref, vec_ref, wq_ref, wkt_ref, wv_ref, wg_ref, wgt_ref, bg_ref, bgt_ref,
                    q_ref, kt_ref, v_ref, gc_ref, gr_ref):
    h = _modulated(x_ref[...], vec_ref).astype(BF16)
    nt = (((1,), (1,)), ((), ()))
    q_ref[...] = jnp.dot(h, wq_ref[...], preferred_element_type=F32).astype(BF16)
    kt = lax.dot_general(wkt_ref[...], h, nt, preferred_element_type=F32)
    kt_ref[...] = (kt * (ML_DK ** -0.5)).astype(BF16)
    v_ref[...] = jnp.dot(h, wv_ref[...], preferred_element_type=F32).astype(BF16)
    gc_ref[...] = jnp.dot(h, wg_ref[...], preferred_element_type=F32) + bg_ref[...]
    gr_ref[...] = lax.dot_general(wgt_ref[...], h, nt, preferred_element_type=F32) + bgt_ref[...]


def _ml_proj_call(x, vec, wq, wkt, wv, wg, wgt, bg, bgt, tm):
    t, d = x.shape
    row = lambda i: (i, 0)
    col = lambda i: (0, i)
    return pl.pallas_call(
        _ml_proj_kernel,
        grid=(t // tm,),
        in_specs=[pl.BlockSpec((tm, d), row), _full((8, d)), _full(wq.shape), _full(wkt.shape), _full(wv.shape),
                  _full(wg.shape), _full(wgt.shape), _full(bg.shape), _full(bgt.shape)],
        out_specs=[pl.BlockSpec((tm, ML_KD), row), pl.BlockSpec((ML_KD, tm), col), pl.BlockSpec((tm, d), row),
                   pl.BlockSpec((tm, ML_GATES), row), pl.BlockSpec((ML_GATES, tm), col)],
        out_shape=[jax.ShapeDtypeStruct((t, ML_KD), BF16), jax.ShapeDtypeStruct((ML_KD, t), BF16),
                   jax.ShapeDtypeStruct((t, d), BF16), jax.ShapeDtypeStruct((t, ML_GATES), F32),
                   jax.ShapeDtypeStruct((ML_GATES, t), F32)],
        compiler_params=_cparams("arbitrary"),
        name="mlstm_in_proj",
    )(x, vec, wq, wkt, wv, wg, wgt, bg, bgt)


def _ml_scan_kernel(qf_ref, ktf_ref, vf_ref, gcf_ref, grf_ref, qb_ref, ktb_ref, vb_ref, gcb_ref, grb_ref,
                    c0_ref, m0_ref, hf_ref, hb_ref, cout_ref, mout_ref, c_ref, m_ref):
    n = pl.program_id(0)
    L = ML_CHUNK

    @pl.when(n == 0)
    def _():
        c_ref[...] = c0_ref[...]
        m_ref[...] = m0_ref[...]

    r = lax.broadcasted_iota(jnp.int32, (L, L), 0)
    s_ = lax.broadcasted_iota(jnp.int32, (L, L), 1)
    lower = r >= s_
    upper = r <= s_
    ones_blk = jnp.ones((L, ML_STATE_W - ML_DV), BF16)
    hi = lax.Precision.HIGHEST
    per_dir = ((qf_ref, ktf_ref, vf_ref, gcf_ref, grf_ref, hf_ref, lower, upper, L - 1),
               (qb_ref, ktb_ref, vb_ref, gcb_ref, grb_ref, hb_ref, upper, lower, 0))
    for d, (q_ref, kt_ref, v_ref, gc_ref, gr_ref, h_ref, seen, seen_t, end) in enumerate(per_dir):
        gcol = gc_ref[...]
        grow = gr_ref[...]
        a_col = jnp.dot(seen.astype(F32), _log_sigmoid(gcol), precision=hi, preferred_element_type=F32)
        a_row = jnp.dot(_log_sigmoid(grow), seen_t.astype(F32), precision=hi, preferred_element_type=F32)
        for hd in range(ML_HEADS):
            ji = (2 * d) * ML_HEADS + hd
            jf = (2 * d + 1) * ML_HEADS + hd
            st = d * ML_HEADS + hd
            a_c = a_col[:, jf:jf + 1]
            a_r = a_row[jf:jf + 1, :]
            b_r = grow[ji:ji + 1, :] - a_r
            a_end = a_r[:, end:end + 1]
            m_prev = m_ref[st:st + 1, 0:1]
            dm = jnp.where(seen, a_c + b_r, -jnp.inf)
            inter = a_c + m_prev
            m_t = jnp.maximum(jnp.max(dm, axis=1, keepdims=True), inter)
            w = jnp.exp(dm - m_t)
            qh = q_ref[:, hd * ML_DK:(hd + 1) * ML_DK]
            kth = kt_ref[hd * ML_DK:(hd + 1) * ML_DK, :]
            sw = jnp.dot(qh, kth, preferred_element_type=F32) * w
            sc = jnp.exp(inter - m_t)
            vp = jnp.concatenate([v_ref[:, hd * ML_DV:(hd + 1) * ML_DV], ones_blk], axis=1)
            cst = c_ref[st]
            hx = (jnp.dot(sw.astype(BF16), vp, preferred_element_type=F32)
                  + sc * jnp.dot(qh, cst.astype(BF16), preferred_element_type=F32))
            den = jnp.maximum(jnp.abs(hx[:, ML_DV:]), jnp.exp(-m_t))
            inv = 1.0 / den
            h_ref[:, hd * ML_DV:hd * ML_DV + 128] = hx[:, 0:128] * inv
            h_ref[:, hd * ML_DV + 128:(hd + 1) * ML_DV] = hx[:, 128:ML_DV] * inv
            w_end = a_end + b_r
            m_new = jnp.maximum(a_end + m_prev, jnp.max(w_end, axis=1, keepdims=True))
            e_r = jnp.exp(w_end - m_new)
            decay = jnp.exp(a_end + m_prev - m_new)
            ke = (kth.astype(F32) * e_r).astype(BF16)
            c_ref[st] = decay * cst + jnp.dot(ke, vp, preferred_element_type=F32)
            m_ref[st:st + 1, :] = jnp.broadcast_to(m_new, (1, 128))

    @pl.when(n == pl.num_programs(0) - 1)
    def _():
        cout_ref[...] = c_ref[...]
        mout_ref[...] = m_ref[...]


def _ml_scan_call(q, kt, v, gc, gr, c0, m0):
    t, d = v.shape
    L = ML_CHUNK
    nc = t // L
    fr = lambda n: (n, 0)
    fc = lambda n: (0, n)
    br = lambda n: (nc - 1 - n, 0)
    bc = lambda n: (0, nc - 1 - n)

    def specs(rm, cm):
        return [pl.BlockSpec((L, ML_KD), rm), pl.BlockSpec((ML_KD, L), cm), pl.BlockSpec((L, d), rm),
                pl.BlockSpec((L, ML_GATES), rm), pl.BlockSpec((ML_GATES, L), cm)]

    nst = 2 * ML_HEADS
    return pl.pallas_call(
        _ml_scan_kernel,
        grid=(nc,),
        in_specs=specs(fr, fc) + specs(br, bc) + [_full(c0.shape), _full(m0.shape)],
        out_specs=[pl.BlockSpec((L, d), fr), pl.BlockSpec((L, d), br), _full(c0.shape), _full(m0.shape)],
        out_shape=[jax.ShapeDtypeStruct((t, d), F32), jax.ShapeDtypeStruct((t, d), F32),
                   jax.ShapeDtypeStruct(c0.shape, F32), jax.ShapeDtypeStruct(m0.shape, F32)],
        scratch_shapes=[pltpu.VMEM((nst, ML_DK, ML_STATE_W), F32), pltpu.VMEM((nst, 128), F32)],
        compiler_params=_cparams("arbitrary"),
        name="mlstm_scan",
    )(q, kt, v, gc, gr, q, kt, v, gc, gr, c0, m0)


def _ml_out_kernel(x_ref, hf_ref, hb_ref, vec_ref, ng_ref, woi_ref, wo_ref, o_ref):
    x = x_ref[...]
    h = _modulated(x, vec_ref).astype(BF16)
    o = jnp.dot(h, woi_ref[...], preferred_element_type=F32)
    hs = hf_ref[...] + hb_ref[...]
    parts = []
    for hd in range(ML_HEADS):
        seg = hs[:, hd * ML_DV:(hd + 1) * ML_DV]
        parts.append(_head_rms(seg, ng_ref[0:1, hd * ML_DV:(hd + 1) * ML_DV]))
    hn = jnp.concatenate(parts, axis=1)
    y = jnp.dot((_sigmoid(o) * hn).astype(BF16), wo_ref[...], preferred_element_type=F32)
    o_ref[...] = x + vec_ref[2:3, :] * y


def _ml_out_call(x, hf, hb, vec, ng8, woi, wo, tm):
    t, d = x.shape
    row = lambda i: (i, 0)
    return pl.pallas_call(
        _ml_out_kernel,
        grid=(t // tm,),
        in_specs=[pl.BlockSpec((tm, d), row), pl.BlockSpec((tm, d), row), pl.BlockSpec((tm, d), row),
                  _full((8, d)), _full((8, d)), _full(woi.shape), _full(wo.shape)],
        out_specs=pl.BlockSpec((tm, d), row),
        out_shape=jax.ShapeDtypeStruct((t, d), F32),
        compiler_params=_cparams("arbitrary"),
        name="mlstm_out",
    )(x, hf, hb, vec, ng8, woi, wo)


def _pad8(rows):
    a = jnp.stack(rows)
    return jnp.concatenate([a, jnp.zeros((8 - a.shape[0], a.shape[1]), a.dtype)], axis=0)


def _rope_tables(n_tok):
    rows = n_tok // GRID_W
    seg = HEAD_DIM // 2
    inv = ROPE_THETA ** (-jnp.arange(seg // 2, dtype=F32) / (seg // 2))
    ang_r = jnp.arange(rows).astype(F32)[:, None] * inv
    ang_c = jnp.arange(GRID_W).astype(F32)[:, None] * inv

    def expand(fr, fc):
        r = jnp.repeat(fr, GRID_W, axis=0)
        c = jnp.tile(fc, (rows, 1))
        return jnp.concatenate([r, r, c, c], axis=-1)

    return expand(jnp.cos(ang_r), jnp.cos(ang_c)), expand(jnp.sin(ang_r), jnp.sin(ang_c))


def _ctx_read_at_or_after(i):
    return any((j % N_MIXERS) != 0 for j in range(i, DEPTH))


def kernel(x, c, ctx, c_ctx, mod_w, mod_b, norm_g, ffn_w13, ffn_w2, conv_w_in, conv_k, conv_w_out,
           attn_w_qkv, attn_q_g, attn_k_g, attn_w_o, mlstm_w_in, mlstm_b_gate, mlstm_norm_g, mlstm_w_o):
    d = D_MODEL
    t_lat = x.shape[1]
    t_ctx = ctx.shape[1]
    tm = 512
    tmc = t_ctx
    lat = x[0]
    cx = ctx[0]

    cvec = _pad8([c[0], c_ctx])
    modall = _mod_call(cvec, mod_w, mod_b)

    m5 = jnp.transpose(modall[:, :2, :].reshape(DEPTH, 2, 3, 3, d), (0, 2, 1, 3, 4))
    g5 = jnp.broadcast_to(norm_g[:, :, None, None, :], (DEPTH, 3, 2, 1, d))
    vecs_all = jnp.concatenate([m5, g5, jnp.zeros((DEPTH, 3, 2, 4, d), F32)], axis=3)

    def vec(i, a, stream):
        return vecs_all[i, a, stream]

    cos, sin = _rope_tables(t_lat)
    w13_all, w2_all = ffn_w13, ffn_w2
    conv_in_all, conv_out_all = conv_w_in.astype(BF16), conv_w_out.astype(BF16)
    counters = [0, 0, 0]
    for i in range(DEPTH):
        kind = i % N_MIXERS
        j = counters[kind]
        counters[kind] += 1
        ctx_out = _ctx_read_at_or_after(i + 1)
        if not _ctx_read_at_or_after(i):
            cx = None
        lat = _ffn_call(lat, vec(i, 0, 0), w13_all, w2_all, i, 0, tm)
        if cx is not None:
            cx = _ffn_call(cx, vec(i, 0, 1), w13_all, w2_all, i, 0, tmc)
        vl, vc = vec(i, 1, 0), vec(i, 1, 1)
        if kind == 0:
            k8 = _pad8([conv_k[j, 0], conv_k[j, 1], conv_k[j, 2]])
            lat = _conv_call(lat, vl, conv_in_all, k8, conv_out_all, j, CONV_TM)
            if ctx_out:
                cx = _conv_call(cx, vc, conv_in_all, k8, conv_out_all, j, tmc)
        elif kind == 1:
            w_qkv, w_o = attn_w_qkv[j], attn_w_o[j].astype(BF16)
            qd, kd = Q_HEADS * HEAD_DIM, KV_HEADS * HEAD_DIM
            wqv_t = jnp.concatenate([w_qkv[:, :qd], w_qkv[:, qd + kd:]], axis=1).T.astype(BF16)
            wk = w_qkv[:, qd:qd + kd].astype(BF16)
            gk8 = _pad8([attn_k_g[j]])
            gq = attn_q_g[j][:, None]
            one, zero = jnp.ones((t_ctx, HEAD_DIM), F32), jnp.zeros((t_ctx, HEAD_DIM), F32)
            q, k, v = _qkv_call(lat, vl, wqv_t, wk, jnp.broadcast_to(gq, (HEAD_DIM, tm)), gk8,
                                cos, sin, cos.T, sin.T, tm, ATT_TQ)
            qc, kc, vc_ = _qkv_call(cx, vc, wqv_t, wk, jnp.broadcast_to(gq, (HEAD_DIM, tmc)), gk8,
                                    one, zero, one.T, zero.T, tmc, t_ctx)
            bound = (HEAD_DIM * QK_SCALE_LOG2) * jnp.max(jnp.abs(attn_q_g[j])) * jnp.max(jnp.abs(attn_k_g[j]))
            att = lax.cond(2.0 * bound <= ATT_LAG_MAX_EXP,
                           lambda *a: _flash_call(*a, lagged=True), lambda *a: _flash_call(*a), q, kc, vc_, k, v)
            lat = _proj_res_call(lat, att, vl, w_o, tm)
            if ctx_out:
                attc = _flash_call(qc, kc, vc_, None, None)
                cx = _proj_res_call(cx, attc, vc, w_o, tmc)
        else:
            w_in = mlstm_w_in[j]
            kd = ML_KD
            g0 = kd + d
            q0 = g0 + ML_GATES
            o0 = q0 + kd
            wkt = w_in[:, :kd].T.astype(BF16)
            wv = w_in[:, kd:g0].astype(BF16)
            wg = w_in[:, g0:q0].astype(BF16)
            wq = w_in[:, q0:o0].astype(BF16)
            woi = w_in[:, o0:].astype(BF16)
            bg = mlstm_b_gate[j].reshape(1, ML_GATES)
            pc = _ml_proj_call(cx, vc, wq, wkt, wv, wg, wg.T, bg, bg.T, tmc)
            pl_ = _ml_proj_call(lat, vl, wq, wkt, wv, wg, wg.T, bg, bg.T, tm)
            c0 = jnp.zeros((2 * ML_HEADS, ML_DK, ML_STATE_W), F32)
            m0 = jnp.zeros((2 * ML_HEADS, 128), F32)
            hfc, hbc, c1, m1 = _ml_scan_call(*pc, c0, m0)
            hf, hb, _, _ = _ml_scan_call(*pl_, c1, m1)
            ng8 = _pad8([mlstm_norm_g[j]])
            wo = mlstm_w_o[j].astype(BF16)
            lat = _ml_out_call(lat, hf, hb, vl, ng8, woi, wo, tm)
            if ctx_out:
                cx = _ml_out_call(cx, hfc, hbc, vc, ng8, woi, wo, tmc)
        if ctx_out:
            cx = _ffn_call(cx, vec(i, 2, 1), w13_all, w2_all, i, 1, tmc)
        else:
            cx = None
        lat = _ffn_call(lat, vec(i, 2, 0), w13_all, w2_all, i, 1, tm)
    return lat[None]
```
